```python
import math
import jax, jax.numpy as jnp
from jax import lax
import numpy as np

D_MODEL = 1024
BATCH = 8
SEQ = 4096
DEPTH = 2

MOBA_HEADS = 8
MOBA_HEAD_DIM = 128
MOBA_WIDTH = MOBA_HEADS * MOBA_HEAD_DIM
MOBA_BLOCK = 256
MOBA_TOPK = 3
MOBA_QUERY_CHUNK = 64
DN_HEADS = 8
DN_KEY_DIM = 128
DN_VALUE_DIM = 128
DN_QK = DN_HEADS * DN_KEY_DIM
DN_V = DN_HEADS * DN_VALUE_DIM
DN_CONV = 4
DN_CHUNK = 64
IN_SIZES = (MOBA_WIDTH, MOBA_WIDTH, MOBA_WIDTH, DN_QK, DN_QK, DN_V, DN_V, DN_HEADS, DN_HEADS, D_MODEL, D_MODEL)
IN_WIDTH = sum(IN_SIZES)
N_EXPERTS = 64
N_GROUPS = 8
TOPK_GROUPS = 4
TOP_K = 8
EXPERT_DIM = 256
SHARED_DIM = 256
ROUTED_SCALE = 2.5
DISPATCH_BLOCK = 128
LN_EPS = 1e-5
RMS_EPS = 1e-6
L2_EPS = 1e-6
DEEPNORM_ALPHA = (2 * DEPTH) ** 0.25
DEEPNORM_BETA = (8 * DEPTH) ** -0.25

kernel_name = "moba_gdn_gated_merge_moe_deepnorm"


def layer_norm(x, g, b):
    xf = x.astype(jnp.float32)
    mu = jnp.mean(xf, axis=-1, keepdims=True)
    xc = xf - mu
    var = jnp.mean(xc * xc, axis=-1, keepdims=True)
    y = xc * lax.rsqrt(var + LN_EPS) * g.astype(jnp.float32) + b.astype(jnp.float32)
    return y.astype(x.dtype)


def l2norm(x):
    return x * lax.rsqrt(jnp.sum(x * x, axis=-1, keepdims=True) + L2_EPS)


def moba_attention(q, k, v):
    B, S, H, Dh = q.shape
    BLK, CQ = MOBA_BLOCK, MOBA_QUERY_CHUNK
    Sp = -(-S // BLK) * BLK
    if Sp != S:
        pad = ((0, 0), (0, Sp - S), (0, 0), (0, 0))
        q, k, v = jnp.pad(q, pad), jnp.pad(k, pad), jnp.pad(v, pad)
    NB = Sp // BLK
    NC = Sp // CQ
    qh = q.transpose(0, 2, 1, 3)
    kb = k.transpose(0, 2, 1, 3).reshape(B, H, NB, BLK, Dh)
    vb = v.transpose(0, 2, 1, 3).reshape(B, H, NB, BLK, Dh)
    kmean = jnp.mean(kb.astype(jnp.float32), axis=3)
    gate = jnp.einsum('bhsd,bhnd->bhsn', qh.astype(jnp.float32), kmean)
    qblk = jnp.arange(Sp) // BLK
    past = jnp.arange(NB)[None, :] < qblk[:, None]
    gate = jnp.where(past, gate, -jnp.inf)
    nsel = min(MOBA_TOPK, NB)
    _, sel = lax.top_k(gate, nsel)
    valid = sel < qblk[None, None, :, None]

    def chunks(t):
        return t.reshape(B, H, NC, CQ, t.shape[-1]).transpose(0, 2, 1, 3, 4).reshape(B * NC, H, CQ, t.shape[-1])

    q_c, sel_c, valid_c = chunks(qh), chunks(sel), chunks(valid)
    b_id = jnp.repeat(jnp.arange(B, dtype=jnp.int32), NC)
    c_id = jnp.tile(jnp.arange(NC, dtype=jnp.int32), B)
    scale = Dh ** -0.5
    key_off = jnp.arange(BLK)
    q_off = jnp.arange(CQ)
    h_ix = jnp.arange(H)[:, None, None]

    def attend(args):
        qc, sc, vc, b, c = args
        kbb, vbb = kb[b], vb[b]
        ksel = kbb[h_ix, sc]
        vsel = vbb[h_ix, sc]
        j = (c * CQ) // BLK
        kown, vown = kbb[:, j], vbb[:, j]
        s_sel = jnp.einsum('hqd,hqnkd->hqnk', qc, ksel, preferred_element_type=jnp.float32) * scale
        s_sel = jnp.where(vc[..., None], s_sel, -jnp.inf).reshape(H, CQ, nsel * BLK)
        s_own = jnp.einsum('hqd,hkd->hqk', qc, kown, preferred_element_type=jnp.float32) * scale
        causal = (j * BLK + key_off)[None, :] <= (c * CQ + q_off)[:, None]
        s_own = jnp.where(causal[None], s_own, -jnp.inf)
        p = jax.nn.softmax(jnp.concatenate([s_sel, s_own], axis=-1), axis=-1)
        p_sel = p[..., :nsel * BLK].reshape(H, CQ, nsel, BLK).astype(vsel.dtype)
        p_own = p[..., nsel * BLK:].astype(vown.dtype)
        return (jnp.einsum('hqnk,hqnkd->hqd', p_sel, vsel)
                + jnp.einsum('hqk,hkd->hqd', p_own, vown))

    out = lax.map(attend, (q_c, sel_c, valid_c, b_id, c_id))
    out = out.reshape(B, NC, H, CQ, Dh).transpose(0, 1, 3, 2, 4).reshape(B, Sp, H, Dh)
    return out[:, :S]


def causal_depthwise_conv(x, w):
    width, channels = w.shape
    return lax.conv_general_dilated(
        x, w[:, None, :].astype(x.dtype), window_strides=(1,), padding=[(width - 1, 0)],
        dimension_numbers=("NWC", "WIO", "NWC"), feature_group_count=channels)


def chunk_gated_delta_rule(q, k, v, g, beta):
    B, S, H, Dk = q.shape
    Dv = v.shape[-1]
    C = DN_CHUNK
    N = S // C

    def chunks(t):
        return t.reshape(B, N, C, H, t.shape[-1]).transpose(0, 3, 1, 2, 4)

    q, k, v = chunks(q), chunks(k), chunks(v)
    gc = jnp.cumsum(g.reshape(B, N, C, H).transpose(0, 3, 1, 2), axis=-1)
    bt = beta.reshape(B, N, C, H).transpose(0, 3, 1, 2)[..., None]
    kb, vb = k * bt, v * bt
    incl = jnp.tril(jnp.ones((C, C), dtype=bool))
    strict = jnp.tril(jnp.ones((C, C), dtype=bool), -1)
    diff = gc[..., :, None] - gc[..., None, :]
    decay = jnp.where(incl, jnp.exp(jnp.where(incl, diff, 0.0)), 0.0)
    L = jnp.where(strict, jnp.einsum('bhnid,bhnjd->bhnij', kb, k) * decay, 0.0)
    eye = jnp.eye(C, dtype=jnp.float32)
    T = lax.linalg.triangular_solve(L + eye, jnp.broadcast_to(eye, L.shape), left_side=True,
                                    lower=True, unit_diagonal=True)
    u = T @ vb
    w = T @ (kb * jnp.exp(gc)[..., None])
    A = jnp.einsum('bhnid,bhnjd->bhnij', q, k) * decay
    qd = q * jnp.exp(gc)[..., None]
    kd = k * jnp.exp(gc[..., -1:] - gc)[..., None]
    glast = jnp.exp(gc[..., -1])
    xs = (jnp.moveaxis(qd, 2, 0), jnp.moveaxis(kd, 2, 0), jnp.moveaxis(u, 2, 0),
          jnp.moveaxis(w, 2, 0), jnp.moveaxis(A, 2, 0), jnp.moveaxis(glast, 2, 0))

    def step(state, inp):
        qd_i, kd_i, u_i, w_i, a_i, gl_i = inp
        v_new = u_i - jnp.einsum('bhcd,bhde->bhce', w_i, state)
        o = jnp.einsum('bhcd,bhde->bhce', qd_i, state) + jnp.einsum('bhcj,bhje->bhce', a_i, v_new)
        state = state * gl_i[..., None, None] + jnp.einsum('bhcd,bhce->bhde', kd_i, v_new)
        return state, o

    state0 = jnp.zeros((B, H, Dk, Dv), jnp.float32)
    _, o = lax.scan(step, state0, xs)
    return o.transpose(1, 0, 3, 2, 4).reshape(B, S, H, Dv)


def gated_delta_net(q, k, v, z, a, b, conv_w, a_log, dt_bias, norm_w):
    Bsz, S, _ = q.shape
    qkv = jax.nn.silu(causal_depthwise_conv(jnp.concatenate([q, k, v], axis=-1), conv_w))
    qc, kc, vc = jnp.split(qkv, [DN_QK, 2 * DN_QK], axis=-1)
    qf = l2norm(qc.reshape(Bsz, S, DN_HEADS, DN_KEY_DIM).astype(jnp.float32)) * (DN_KEY_DIM ** -0.5)
    kf = l2norm(kc.reshape(Bsz, S, DN_HEADS, DN_KEY_DIM).astype(jnp.float32))
    vf = vc.reshape(Bsz, S, DN_HEADS, DN_VALUE_DIM).astype(jnp.float32)
    beta = jax.nn.sigmoid(b.astype(jnp.float32))
    g = -jnp.exp(a_log.astype(jnp.float32)) * jax.nn.softplus(a.astype(jnp.float32) + dt_bias.astype(jnp.float32))
    o = chunk_gated_delta_rule(qf, kf, vf, g, beta)
    zf = z.reshape(Bsz, S, DN_HEADS, DN_VALUE_DIM).astype(jnp.float32)
    o = o * lax.rsqrt(jnp.mean(o * o, axis=-1, keepdims=True) + RMS_EPS) * norm_w.astype(jnp.float32) * jax.nn.silu(zf)
    return o.reshape(Bsz, S, DN_V).astype(q.dtype)


def token_mixer(x, w_in, conv_w, a_log, dt_bias, norm_w, w_branch_a, w_branch_b, w_out):
    Bsz, S, _ = x.shape
    proj = x @ w_in
    (q_a, k_a, v_a, q_b, k_b, v_b, z_b, a_b, b_b, gate_a, gate_b) = jnp.split(
        proj, np.cumsum(IN_SIZES)[:-1].tolist(), axis=-1)
    heads_a = lambda t: t.reshape(Bsz, S, MOBA_HEADS, MOBA_HEAD_DIM)
    y_a = moba_attention(heads_a(q_a), heads_a(k_a), heads_a(v_a)).reshape(Bsz, S, MOBA_WIDTH)
    y_b = gated_delta_net(q_b, k_b, v_b, z_b, a_b, b_b, conv_w, a_log, dt_bias, norm_w)
    merged = jax.nn.sigmoid(gate_a) * (y_a @ w_branch_a) + jax.nn.sigmoid(gate_b) * (y_b @ w_branch_b)
    return merged @ w_out


def swiglu(x, w_gate, w_up, w_down):
    return (jax.nn.silu(x @ w_gate) * (x @ w_up)) @ w_down


def moe_ffn(x, w_router, router_bias, w_gate_e, w_up_e, w_down_e, w_gate_s, w_up_s, w_down_s):
    Bsz, S, D = x.shape
    xf = x.reshape(-1, D)
    N = xf.shape[0]
    scores = jax.nn.sigmoid(jnp.dot(xf, w_router, preferred_element_type=jnp.float32))
    choice = scores + router_bias.astype(jnp.float32)
    grouped = choice.reshape(N, N_GROUPS, N_EXPERTS // N_GROUPS)
    group_score = jnp.sum(lax.top_k(grouped, 2)[0], axis=-1)
    _, gidx = lax.top_k(group_score, TOPK_GROUPS)
    gmask = jnp.any(gidx[..., None] == jnp.arange(N_GROUPS), axis=1)
    emask = jnp.repeat(gmask, N_EXPERTS // N_GROUPS, axis=1)
    _, eidx = lax.top_k(jnp.where(emask, choice, -jnp.inf), TOP_K)
    wts = jnp.take_along_axis(scores, eidx, axis=1)
    wts = wts / jnp.sum(wts, axis=-1, keepdims=True) * ROUTED_SCALE

    NK = N * TOP_K
    BLK = DISPATCH_BLOCK
    nb = -(-(NK + N_EXPERTS * (BLK - 1)) // BLK)
    P = nb * BLK
    flat_e = eidx.reshape(-1)
    flat_tok = jnp.arange(NK, dtype=jnp.int32) // TOP_K
    flat_w = wts.reshape(-1)
    order = jnp.argsort(flat_e)
    e_sorted = flat_e[order]
    counts = jnp.bincount(flat_e, length=N_EXPERTS)
    padded = (counts + BLK - 1) // BLK * BLK
    start = jnp.cumsum(counts) - counts
    pend = jnp.cumsum(padded)
    pstart = pend - padded
    dest = pstart[e_sorted] + (jnp.arange(NK) - start[e_sorted])
    tok_pad = jnp.full((P,), N, jnp.int32).at[dest].set(flat_tok[order])
    w_pad = jnp.zeros((P,), jnp.float32).at[dest].set(flat_w[order])
    blk_e = jnp.clip(jnp.searchsorted(pend, jnp.arange(nb) * BLK, side='right'), 0, N_EXPERTS - 1)
    x_ext = jnp.concatenate([xf, jnp.zeros((1, D), xf.dtype)], axis=0)

    def expert_block(acc, inp):
        tok, wt, e = inp
        xb = x_ext[tok]
        yb = swiglu(xb, w_gate_e[e], w_up_e[e], w_down_e[e]) * wt[:, None].astype(xb.dtype)
        return acc.at[tok].add(yb), None

    acc, _ = lax.scan(expert_block, jnp.zeros((N + 1, D), x.dtype),
                      (tok_pad.reshape(nb, BLK), w_pad.reshape(nb, BLK), blk_e))
    out = acc[:N] + swiglu(xf, w_gate_s, w_up_s, w_down_s)
    return out.reshape(Bsz, S, D)


def setup_inputs(seed: int = 0) -> dict:
    key = jax.random.key(seed)
    ks = jax.random.split(key, 24)

    def nrm(k, shape, scale):
        return jax.random.normal(k, shape, jnp.float32) * scale

    dt = jnp.exp(jax.random.uniform(ks[4], (DEPTH, DN_HEADS), jnp.float32, math.log(1e-3), math.log(1e-1)))
    return {
        "x": nrm(ks[0], (BATCH, SEQ, D_MODEL), 1.0),
        "w_in": nrm(ks[1], (DEPTH, D_MODEL, IN_WIDTH), D_MODEL ** -0.5),
        "dn_conv_w": nrm(ks[2], (DEPTH, DN_CONV, 2 * DN_QK + DN_V), DN_CONV ** -0.5),
        "dn_a_log": jnp.log(jax.random.uniform(ks[3], (DEPTH, DN_HEADS), jnp.float32, 1.0, 16.0)),
        "dn_dt_bias": jnp.log(jnp.expm1(dt)),
        "dn_norm_w": 1.0 + nrm(ks[5], (DEPTH, DN_VALUE_DIM), 0.02),
        "w_branch_a": nrm(ks[6], (DEPTH, MOBA_WIDTH, D_MODEL), MOBA_WIDTH ** -0.5),
        "w_branch_b": nrm(ks[7], (DEPTH, DN_V, D_MODEL), DN_V ** -0.5),
        "w_out": nrm(ks[8], (DEPTH, D_MODEL, D_MODEL), D_MODEL ** -0.5 * DEEPNORM_BETA),
        "ln1_g": 1.0 + nrm(ks[9], (DEPTH, D_MODEL), 0.02),
        "ln1_b": nrm(ks[10], (DEPTH, D_MODEL), 0.02),
        "w_router": nrm(ks[11], (DEPTH, D_MODEL, N_EXPERTS), D_MODEL ** -0.5),
        "router_bias": nrm(ks[12], (DEPTH, N_EXPERTS), 0.01),
        "w_gate_e": nrm(ks[13], (DEPTH, N_EXPERTS, D_MODEL, EXPERT_DIM), D_MODEL ** -0.5),
        "w_up_e": nrm(ks[14], (DEPTH, N_EXPERTS, D_MODEL, EXPERT_DIM), D_MODEL ** -0.5),
        "w_down_e": nrm(ks[15], (DEPTH, N_EXPERTS, EXPERT_DIM, D_MODEL), EXPERT_DIM ** -0.5 * DEEPNORM_BETA),
        "w_gate_s": nrm(ks[16], (DEPTH, D_MODEL, SHARED_DIM), D_MODEL ** -0.5),
        "w_up_s": nrm(ks[17], (DEPTH, D_MODEL, SHARED_DIM), D_MODEL ** -0.5),
        "w_down_s": nrm(ks[18], (DEPTH, SHARED_DIM, D_MODEL), SHARED_DIM ** -0.5 * DEEPNORM_BETA),
        "ln2_g": 1.0 + nrm(ks[19], (DEPTH, D_MODEL), 0.02),
        "ln2_b": nrm(ks[20], (DEPTH, D_MODEL), 0.02),
    }


def reference(x, w_in, dn_conv_w, dn_a_log, dn_dt_bias, dn_norm_w, w_branch_a, w_branch_b, w_out,
              ln1_g, ln1_b, w_router, router_bias, w_gate_e, w_up_e, w_down_e,
              w_gate_s, w_up_s, w_down_s, ln2_g, ln2_b):
    for l in range(DEPTH):
        mix = token_mixer(x, w_in[l], dn_conv_w[l], dn_a_log[l], dn_dt_bias[l], dn_norm_w[l],
                          w_branch_a[l], w_branch_b[l], w_out[l])
        x = layer_norm(DEEPNORM_ALPHA * x + mix, ln1_g[l], ln1_b[l])
        ffn = moe_ffn(x, w_router[l], router_bias[l], w_gate_e[l], w_up_e[l], w_down_e[l],
                      w_gate_s[l], w_up_s[l], w_down_s[l])
        x = layer_norm(DEEPNORM_ALPHA * x + ffn, ln2_g[l], ln2_b[l])
    return x
```

```python
import functools

import jax
import jax.numpy as jnp
from jax import lax
from jax.experimental import pallas as pl
from jax.experimental.pallas import tpu as pltpu

D_MODEL = 1024
N_HEADS = 8
HEAD_DIM = 128
MOBA_BLOCK = 256
MOBA_TOPK = 3
DN_CONV = 4
DN_CHUNK = 64
N_EXPERTS = 64
N_GROUPS = 8
TOPK_GROUPS = 4
TOP_K = 8
EXPERT_DIM = 256
ROUTED_SCALE = 2.5
LN_EPS = 1e-5
RMS_EPS = 1e-6
L2_EPS = 1e-6
DEPTH = 2
DEEPNORM_ALPHA = (2 * DEPTH) ** 0.25

LANES = 128
SUBLANES = 8
VMEM_LIMIT = 48 * 1024 * 1024

MM_TM = 1024
MM_TN = 1024
GDN_GROUP = 128
PREP_T = 256
MIX_T = 512
ROUTE_T = 512
EXP_BLK = 256
COMB_T = 128

F32 = jnp.float32
BF16 = jnp.bfloat16
NEG_INF = float("-inf")
HI = lax.Precision.HIGHEST


def _cparams(*sem):
    return pltpu.CompilerParams(dimension_semantics=sem, vmem_limit_bytes=VMEM_LIMIT)


def _dot(a, b):
    return jnp.dot(a, b, preferred_element_type=F32)


def _dot_nt(a, b, precision=None):
    return lax.dot_general(a, b, (((1,), (1,)), ((), ())), precision=precision,
                           preferred_element_type=F32)


def _sigmoid(x):
    return 1.0 / (1.0 + jnp.exp(-x))


def _silu(x):
    return x * _sigmoid(x)


def _mm_kernel(a_ref, b_ref, o_ref):
    o_ref[...] = _dot(a_ref[...], b_ref[...]).astype(o_ref.dtype)


def _matmul(a, b, out_dtype, tm, tn):
    m, k = a.shape
    _, n = b.shape
    return pl.pallas_call(
        _mm_kernel,
        grid=(m // tm, n // tn),
        in_specs=[pl.BlockSpec((tm, k), lambda i, j: (i, 0)),
                  pl.BlockSpec((k, tn), lambda i, j: (0, j))],
        out_specs=pl.BlockSpec((tm, tn), lambda i, j: (i, j)),
        out_shape=jax.ShapeDtypeStruct((m, n), out_dtype),
        compiler_params=_cparams("parallel", "parallel"),
        name="inproj",
    )(a, b)


def _moba_kernel(q_ref, k_ref, v_ref, o_ref, kmean_ref, *, nb):
    j = pl.program_id(2)
    blk = MOBA_BLOCK
    scale = HEAD_DIM ** -0.5

    @pl.when(j == 0)
    def _():
        kmean_ref[...] = jnp.zeros_like(kmean_ref)
        for n in range(nb):
            kb = k_ref[n * blk:(n + 1) * blk, :].astype(F32)
            kmean_ref[n:n + 1, :] = jnp.mean(kb, axis=0, keepdims=True)

    q = q_ref[...]
    gate = _dot_nt(q.astype(F32), kmean_ref[...], precision=HI)
    col = lax.broadcasted_iota(jnp.int32, gate.shape, 1)
    gate = jnp.where(col < j, gate, NEG_INF)
    picks = []
    for _ in range(MOBA_TOPK):
        m = jnp.max(gate, axis=-1, keepdims=True)
        idx = jnp.min(jnp.where(gate == m, col, LANES), axis=-1, keepdims=True)
        picks.append(idx)
        gate = jnp.where(col == idx, NEG_INF, gate)

    row_i = lax.broadcasted_iota(jnp.int32, (blk, blk), 0)
    col_i = lax.broadcasted_iota(jnp.int32, (blk, blk), 1)
    j0 = pl.multiple_of(j * blk, blk)
    s = _dot_nt(q, k_ref[pl.ds(j0, blk), :]) * scale
    s = jnp.where(col_i <= row_i, s, NEG_INF)
    m0 = jnp.max(s, axis=-1, keepdims=True)
    p = jnp.exp(s - m0)
    l0 = jnp.sum(p, axis=-1, keepdims=True)
    acc0 = _dot(p.astype(BF16), v_ref[pl.ds(j0, blk), :])

    def body(n, carry):
        m_i, l_i, acc = carry
        n0 = pl.multiple_of(n * blk, blk)
        sel = (picks[0] == n) | (picks[1] == n) | (picks[2] == n)
        s = _dot_nt(q, k_ref[pl.ds(n0, blk), :]) * scale
        s = jnp.where(sel, s, NEG_INF)
        m_new = jnp.maximum(m_i, jnp.max(s, axis=-1, keepdims=True))
        alpha = jnp.exp(m_i - m_new)
        p = jnp.exp(s - m_new)
        l_new = alpha * l_i + jnp.sum(p, axis=-1, keepdims=True)
        acc_new = alpha * acc + _dot(p.astype(BF16), v_ref[pl.ds(n0, blk), :])
        return m_new, l_new, acc_new

    _, l_f, acc_f = lax.fori_loop(0, j, body, (m0, l0, acc0))
    o_ref[...] = (acc_f / l_f).astype(o_ref.dtype)


def _moba(proj, bsz, seq):
    nb = seq // MOBA_BLOCK
    hpd = D_MODEL // HEAD_DIM
    return pl.pallas_call(
        functools.partial(_moba_kernel, nb=nb),
        grid=(bsz, N_HEADS, nb),
        in_specs=[
            pl.BlockSpec((MOBA_BLOCK, HEAD_DIM), lambda b, h, j: (b * nb + j, h)),
            pl.BlockSpec((seq, HEAD_DIM), lambda b, h, j: (b, hpd + h)),
            pl.BlockSpec((seq, HEAD_DIM), lambda b, h, j: (b, 2 * hpd + h)),
        ],
        out_specs=pl.BlockSpec((MOBA_BLOCK, HEAD_DIM), lambda b, h, j: (b * nb + j, h)),
        out_shape=jax.ShapeDtypeStruct((bsz * seq, D_MODEL), BF16),
        scratch_shapes=[pltpu.VMEM((LANES, HEAD_DIM), F32)],
        compiler_params=_cparams("parallel", "parallel", "arbitrary"),
        name="moba",
    )(proj, proj, proj)


def _gdn_prep_kernel(q_ref, k_ref, v_ref, ab_ref, cw_ref, par_ref,
                     qn_ref, kn_ref, vc_ref, gcb_ref, xbuf_ref, *, tile):
    t = pl.program_id(1)
    halo = SUBLANES

    @pl.when(t == 0)
    def _():
        xbuf_ref[0:halo, :] = jnp.zeros((halo, 3 * D_MODEL), F32)

    @pl.when(t > 0)
    def _():
        xbuf_ref[0:halo, :] = xbuf_ref[tile:tile + halo, :]

    for c, src in enumerate((q_ref, k_ref, v_ref)):
        xbuf_ref[halo:halo + tile, c * D_MODEL:(c + 1) * D_MODEL] = src[...].astype(F32)

    outs = (qn_ref, kn_ref, vc_ref)
    for c in range(3):
        for h in range(N_HEADS):
            lo = c * D_MODEL + h * HEAD_DIM
            acc = None
            for i in range(DN_CONV):
                off = halo - (DN_CONV - 1) + i
                term = xbuf_ref[off:off + tile, lo:lo + HEAD_DIM] * cw_ref[i:i + 1, lo:lo + HEAD_DIM]
                acc = term if acc is None else acc + term
            y = _silu(acc)
            if c < 2:
                y = y * lax.rsqrt(jnp.sum(y * y, axis=-1, keepdims=True) + L2_EPS)
                if c == 0:
                    y = y * (HEAD_DIM ** -0.5)
            outs[c][:, h * HEAD_DIM:(h + 1) * HEAD_DIM] = y.astype(outs[c].dtype)

    ab = ab_ref[...]
    a_log = par_ref[0:1, :]
    dt_bias = par_ref[1:2, :]
    xa = ab + dt_bias
    softplus = jnp.maximum(xa, 0.0) + jnp.log(1.0 + jnp.exp(-jnp.abs(xa)))
    g = -jnp.exp(a_log) * softplus
    ri = lax.broadcasted_iota(jnp.int32, (tile, tile), 0)
    ci = lax.broadcasted_iota(jnp.int32, (tile, tile), 1)
    shift = DN_CHUNK.bit_length() - 1
    tri = ((ci <= ri) & ((ri >> shift) == (ci >> shift))).astype(F32)
    gc = jnp.dot(tri, g, precision=HI, preferred_element_type=F32)
    lane = lax.broadcasted_iota(jnp.int32, ab.shape, 1)
    gcb_ref[...] = jnp.where(lane < N_HEADS, gc, _sigmoid(ab))


def _gdn_prep(proj, ab, conv_w, par, bsz, seq):
    nt = seq // PREP_T
    n = bsz * seq
    row = lambda b, t: (b * nt + t, 0)
    big = pl.BlockSpec((PREP_T, D_MODEL), row)
    return pl.pallas_call(
        functools.partial(_gdn_prep_kernel, tile=PREP_T),
        grid=(bsz, nt),
        in_specs=[
            pl.BlockSpec((PREP_T, D_MODEL), lambda b, t: (b * nt + t, 3)),
            pl.BlockSpec((PREP_T, D_MODEL), lambda b, t: (b * nt + t, 4)),
            pl.BlockSpec((PREP_T, D_MODEL), lambda b, t: (b * nt + t, 5)),
            pl.BlockSpec((PREP_T, LANES), row),
            pl.BlockSpec((DN_CONV, 3 * D_MODEL), lambda b, t: (0, 0)),
            pl.BlockSpec((SUBLANES, LANES), lambda b, t: (0, 0)),
        ],
        out_specs=[big, big, big, pl.BlockSpec((PREP_T, LANES), row)],
        out_shape=[jax.ShapeDtypeStruct((n, D_MODEL), BF16)] * 3
        + [jax.ShapeDtypeStruct((n, LANES), F32)],
        scratch_shapes=[pltpu.VMEM((PREP_T + SUBLANES, 3 * D_MODEL), F32)],
        compiler_params=_cparams("parallel", "arbitrary"),
        name="gdn_prep",
    )(proj, proj, proj, ab, conv_w, par)


def _split(a):
    hi = a.astype(BF16)
    lo = (a - hi.astype(F32)).astype(BF16)
    return hi, lo


def _mm3(a, b):
    ah, al = _split(a)
    bh, bl = _split(b)
    return _dot(ah, bh) + _dot(ah, bl) + _dot(al, bh)


def _gdn_wy_kernel(qn_ref, kn_ref, vc_ref, gcb_ref,
                   u_ref, w_ref, qd_ref, kdt_ref, a_ref, gl_ref):
    rows = GDN_GROUP
    gcb = gcb_ref[...]
    gcb_t = gcb.T
    ri = lax.broadcasted_iota(jnp.int32, (rows, rows), 0)
    ci = lax.broadcasted_iota(jnp.int32, (rows, rows), 1)
    shift = DN_CHUNK.bit_length() - 1
    same = (ri >> shift) == (ci >> shift)
    incl = same & (ci <= ri)
    strict = same & (ci < ri)
    eye = ri == ci
    eye_bf = eye.astype(BF16)
    rcol = lax.broadcasted_iota(jnp.int32, (rows, 1), 0)
    n_chunks = rows // DN_CHUNK

    for h in range(N_HEADS):
        sl = slice(h * HEAD_DIM, (h + 1) * HEAD_DIM)
        gcol = gcb[:, h:h + 1]
        bcol = gcb[:, N_HEADS + h:N_HEADS + h + 1]
        grow = gcb_t[h:h + 1, :]
        decay = jnp.where(incl, jnp.exp(jnp.where(incl, gcol - grow, 0.0)), 0.0)
        q = qn_ref[:, sl].astype(F32)
        k = kn_ref[:, sl].astype(F32)
        v = vc_ref[:, sl].astype(F32)
        kb = k * bcol
        eg = jnp.exp(gcol)
        glast = gcb[DN_CHUNK - 1:DN_CHUNK, h:h + 1]
        gl_ref[n_chunks * h:n_chunks * h + 1, :] = jnp.broadcast_to(glast, (1, LANES))
        glrow = jnp.broadcast_to(glast, (rows, 1))
        for c in range(1, n_chunks):
            gl_c = gcb[(c + 1) * DN_CHUNK - 1:(c + 1) * DN_CHUNK, h:h + 1]
            gl_ref[n_chunks * h + c:n_chunks * h + c + 1, :] = jnp.broadcast_to(gl_c, (1, LANES))
            glrow = jnp.where(rcol >= c * DN_CHUNK, gl_c, glrow)

        k_bf = k.astype(BF16)
        lmat = jnp.where(strict, _dot_nt(kb.astype(BF16), k_bf) * decay, 0.0)
        x = -lmat
        tmat = jnp.where(eye, 1.0, x)
        for _ in range(shift - 1):
            x = _mm3(x, x)
            tmat = tmat + _mm3(tmat, x)
        rhs = jnp.concatenate([v * bcol, kb * eg], axis=1).astype(BF16)
        th, tl = _split(tmat)
        uw = _dot(th, rhs) + _dot(tl, rhs)
        u_ref[:, sl] = uw[:, :HEAD_DIM].astype(u_ref.dtype)
        w_ref[:, sl] = uw[:, HEAD_DIM:].astype(w_ref.dtype)
        qd_ref[:, sl] = (q * eg).astype(qd_ref.dtype)
        kd = (k * jnp.exp(glrow - gcol)).astype(BF16)
        kdt_ref[:, sl] = _dot_nt(eye_bf, kd).astype(kdt_ref.dtype)
        a_ref[:, sl] = jnp.where(incl, _dot_nt(q.astype(BF16), k_bf) * decay, 0.0).astype(a_ref.dtype)


def _gdn_wy(qn, kn, vc, gcb, bsz, seq):
    ng = seq // GDN_GROUP
    n = bsz * seq
    row = lambda b, g: (b * ng + g, 0)
    big = pl.BlockSpec((GDN_GROUP, D_MODEL), row)
    gl_rows = N_HEADS * (GDN_GROUP // DN_CHUNK)
    return pl.pallas_call(
        _gdn_wy_kernel,
        grid=(bsz, ng),
        in_specs=[big, big, big, pl.BlockSpec((GDN_GROUP, LANES), row)],
        out_specs=[big] * 5 + [pl.BlockSpec((gl_rows, LANES), row)],
        out_shape=[jax.ShapeDtypeStruct((n, D_MODEL), BF16)] * 5
        + [jax.ShapeDtypeStruct((bsz * ng * gl_rows, LANES), F32)],
        compiler_params=_cparams("parallel", "parallel"),
        name="gdn_wy",
    )(qn, kn, vc, gcb)


def _gdn_scan_kernel(u_ref, w_ref, qd_ref, kdt_ref, a_ref, gl_ref, z_ref, nw_ref,
                     y_ref, state_ref):
    g = pl.program_id(1)
    c = DN_CHUNK

    @pl.when(g == 0)
    def _():
        state_ref[...] = jnp.zeros_like(state_ref)

    zeros = jnp.zeros((c, HEAD_DIM), F32)
    nw = nw_ref[...]
    for h in range(N_HEADS):
        sl = slice(h * HEAD_DIM, (h + 1) * HEAD_DIM)
        s0 = state_ref[h]
        u = u_ref[:, sl].astype(F32)
        w = w_ref[:, sl]
        qd = qd_ref[:, sl]
        kdt = kdt_ref[:, sl]
        gl0 = jnp.exp(gl_ref[2 * h:2 * h + 1, :])
        gl1 = jnp.exp(gl_ref[2 * h + 1:2 * h + 2, :])
        s0b = s0.astype(BF16)
        vn0 = u[:c] - _dot(w[:c], s0b)
        s1 = s0 * gl0 + _dot(kdt, jnp.concatenate([vn0, zeros], axis=0).astype(BF16))
        s1b = s1.astype(BF16)
        vn1 = u[c:] - _dot(w[c:], s1b)
        s2 = s1 * gl1 + _dot(kdt, jnp.concatenate([zeros, vn1], axis=0).astype(BF16))
        state_ref[h] = s2
        vn = jnp.concatenate([vn0, vn1], axis=0).astype(BF16)
        o = jnp.concatenate([_dot(qd[:c], s0b), _dot(qd[c:], s1b)], axis=0) + _dot(a_ref[:, sl], vn)
        ms = jnp.mean(o * o, axis=-1, keepdims=True)
        y = o * lax.rsqrt(ms + RMS_EPS) * nw * _silu(z_ref[:, sl].astype(F32))
        y_ref[:, sl] = y.astype(y_ref.dtype)


def _gdn_scan(u, w, qd, kdt, a, gl, proj, norm_w, bsz, seq):
    assert GDN_GROUP == 2 * DN_CHUNK
    ng = seq // GDN_GROUP
    n = bsz * seq
    row = lambda b, g: (b * ng + g, 0)
    big = pl.BlockSpec((GDN_GROUP, D_MODEL), row)
    gl_rows = N_HEADS * (GDN_GROUP // DN_CHUNK)
    return pl.pallas_call(
        _gdn_scan_kernel,
        grid=(bsz, ng),
        in_specs=[big] * 5 + [
            pl.BlockSpec((gl_rows, LANES), row),
            pl.BlockSpec((GDN_GROUP, D_MODEL), lambda b, g: (b * ng + g, 6)),
            pl.BlockSpec((1, HEAD_DIM), lambda b, g: (0, 0)),
        ],
        out_specs=big,
        out_shape=jax.ShapeDtypeStruct((n, D_MODEL), BF16),
        scratch_shapes=[pltpu.VMEM((N_HEADS, HEAD_DIM, HEAD_DIM), F32)],
        compiler_params=_cparams("parallel", "arbitrary"),
        name="gdn_scan",
    )(u, w, qd, kdt, a, gl, proj, norm_w)


def _gdn(proj, ab, conv_w, a_log, dt_bias, norm_w, bsz, seq):
    par = jnp.zeros((SUBLANES, LANES), F32)
    par = par.at[0, :N_HEADS].set(a_log.astype(F32)).at[1, :N_HEADS].set(dt_bias.astype(F32))
    qn, kn, vc, gcb = _gdn_prep(proj, ab, conv_w.astype(F32), par, bsz, seq)
    u, w, qd, kdt, a, gl = _gdn_wy(qn, kn, vc, gcb, bsz, seq)
    return _gdn_scan(u, w, qd, kdt, a, gl, proj, norm_w.astype(F32).reshape(1, HEAD_DIM), bsz, seq)


def _layer_norm(x, g, b):
    mu = jnp.mean(x, axis=-1, keepdims=True)
    xc = x - mu
    var = jnp.mean(xc * xc, axis=-1, keepdims=True)
    return xc * lax.rsqrt(var + LN_EPS) * g + b


def _mix_kernel(ya_ref, yb_ref, ga_ref, gb_ref, x_ref, wa_ref, wb_ref, wo_ref, g_ref, b_ref,
                xo_ref):
    ma = _dot(ya_ref[...], wa_ref[...])
    mb = _dot(yb_ref[...], wb_ref[...])
    merged = _sigmoid(ga_ref[...].astype(F32)) * ma + _sigmoid(gb_ref[...].astype(F32)) * mb
    mix = _dot(merged.astype(BF16), wo_ref[...])
    xo_ref[...] = _layer_norm(DEEPNORM_ALPHA * x_ref[...] + mix, g_ref[...], b_ref[...])


def _mix(ya, yb, proj, x, wa, wb, wo, g, b):
    n = x.shape[0]
    row = lambda i: (i, 0)
    const = lambda i: (0, 0)
    tile = pl.BlockSpec((MIX_T, D_MODEL), row)
    wspec = pl.BlockSpec((D_MODEL, D_MODEL), const)
    vec = pl.BlockSpec((1, D_MODEL), const)
    return pl.pallas_call(
        _mix_kernel,
        grid=(n // MIX_T,),
        in_specs=[tile, tile,
                  pl.BlockSpec((MIX_T, D_MODEL), lambda i: (i, 7)),
                  pl.BlockSpec((MIX_T, D_MODEL), lambda i: (i, 8)),
                  tile, wspec, wspec, wspec, vec, vec],
        out_specs=tile,
        out_shape=jax.ShapeDtypeStruct((n, D_MODEL), F32),
        compiler_params=_cparams("parallel"),
        name="mix",
    )(ya, yb, proj, proj, x, wa, wb, wo, g, b)


def _first_argmax(vals, iota, size):
    m = jnp.max(vals, axis=0, keepdims=True)
    idx = jnp.min(jnp.where(vals == m, iota, size), axis=0, keepdims=True)
    return m, idx


def _router_kernel(x_ref, wr_ref, bias_ref, eidx_ref, wts_ref):
    tile = x_ref.shape[0]
    gsz = N_EXPERTS // N_GROUPS
    logits = _dot_nt(wr_ref[...], x_ref[...], precision=HI)
    scores = _sigmoid(logits)
    choice = scores + bias_ref[...]
    iota_g = lax.broadcasted_iota(jnp.int32, (gsz, tile), 0)
    iota_e = lax.broadcasted_iota(jnp.int32, (N_EXPERTS, tile), 0)

    gscores = []
    for gi in range(N_GROUPS):
        sub = choice[gi * gsz:(gi + 1) * gsz, :]
        m1, i1 = _first_argmax(sub, iota_g, gsz)
        m2 = jnp.max(jnp.where(iota_g == i1, NEG_INF, sub), axis=0, keepdims=True)
        gscores.append(m1 + m2)
    gs = jnp.concatenate(gscores, axis=0)
    iota_n = lax.broadcasted_iota(jnp.int32, (N_GROUPS, tile), 0)
    gmask = jnp.zeros((N_GROUPS, tile), jnp.bool_)
    for _ in range(TOPK_GROUPS):
        _, gi = _first_argmax(gs, iota_n, N_GROUPS)
        hit = iota_n == gi
        gmask = gmask | hit
        gs = jnp.where(hit, NEG_INF, gs)
    gmask_f = gmask.astype(F32)
    emask = jnp.concatenate(
        [jnp.broadcast_to(gmask_f[gi:gi + 1, :], (gsz, tile)) for gi in range(N_GROUPS)], axis=0) > 0.5

    masked = jnp.where(emask, choice, NEG_INF)
    idxs, ws = [], []
    for _ in range(TOP_K):
        _, ei = _first_argmax(masked, iota_e, N_EXPERTS)
        hit = iota_e == ei
        idxs.append(ei)
        ws.append(jnp.sum(jnp.where(hit, scores, 0.0), axis=0, keepdims=True))
        masked = jnp.where(hit, NEG_INF, masked)
    wsum = ws[0]
    for w in ws[1:]:
        wsum = wsum + w
    eidx_ref[...] = jnp.concatenate(idxs, axis=0)
    wts_ref[...] = jnp.concatenate(ws, axis=0) / wsum * ROUTED_SCALE


def _router(x, wr_t, bias_col):
    n = x.shape[0]
    return pl.pallas_call(
        _router_kernel,
        grid=(n // ROUTE_T,),
        in_specs=[pl.BlockSpec((ROUTE_T, D_MODEL), lambda i: (i, 0)),
                  pl.BlockSpec((N_EXPERTS, D_MODEL), lambda i: (0, 0)),
                  pl.BlockSpec((N_EXPERTS, 1), lambda i: (0, 0))],
        out_specs=[pl.BlockSpec((TOP_K, ROUTE_T), lambda i: (0, i)),
                   pl.BlockSpec((TOP_K, ROUTE_T), lambda i: (0, i))],
        out_shape=[jax.ShapeDtypeStruct((TOP_K, n), jnp.int32),
                   jax.ShapeDtypeStruct((TOP_K, n), F32)],
        compiler_params=_cparams("parallel"),
        name="router",
    )(x, wr_t, bias_col)


def _row_copy(src_hbm, row, dst_buf, slot, r, sem):
    return pltpu.make_async_copy(src_hbm.at[pl.ds(row, 1)], dst_buf.at[slot, pl.ds(r, 1)], sem.at[slot])


def _gather_start(idx_ref, src_hbm, dst_buf, slot, sem, count):
    def body(r, carry):
        _row_copy(src_hbm, idx_ref[0, 0, r], dst_buf, slot, r, sem).start()
        return carry
    lax.fori_loop(0, count, body, 0, unroll=8)


def _gather_wait(src_hbm, dst_buf, slot, sem, count):
    def body(r, carry):
        _row_copy(src_hbm, 0, dst_buf, slot, r, sem).wait()
        return carry
    lax.fori_loop(0, count, body, 0, unroll=8)


def _expert_kernel(blk_e_ref, tok_cur_ref, tok_nxt_ref, x_hbm, wg_ref, wu_ref, wd_ref,
                   y_ref, xbuf, sem):
    del blk_e_ref
    i = pl.program_id(0)
    nblk = pl.num_programs(0)
    slot = lax.rem(i, 2)

    @pl.when(i == 0)
    def _():
        _gather_start(tok_cur_ref, x_hbm, xbuf, 0, sem, EXP_BLK)

    @pl.when(i + 1 < nblk)
    def _():
        _gather_start(tok_nxt_ref, x_hbm, xbuf, 1 - slot, sem, EXP_BLK)

    _gather_wait(x_hbm, xbuf, slot, sem, EXP_BLK)
    xb = xbuf[slot].astype(BF16)
    hmid = _silu(_dot(xb, wg_ref[...])) * _dot(xb, wu_ref[...])
    y_ref[...] = _dot(hmid.astype(BF16), wd_ref[...])


def _experts(blk_e, tok_pad, x, wg, wu, wd):
    nblk = blk_e.shape[0]
    tok3 = tok_pad.reshape(nblk, 1, EXP_BLK)
    smem_blk = lambda f: pl.BlockSpec((1, 1, EXP_BLK), f, memory_space=pltpu.SMEM)
    grid_spec = pltpu.PrefetchScalarGridSpec(
        num_scalar_prefetch=1,
        grid=(nblk,),
        in_specs=[
            smem_blk(lambda i, be: (i, 0, 0)),
            smem_blk(lambda i, be: (jnp.minimum(i + 1, nblk - 1), 0, 0)),
            pl.BlockSpec(memory_space=pl.ANY),
            pl.BlockSpec((None, D_MODEL, EXPERT_DIM), lambda i, be: (be[i], 0, 0)),
            pl.BlockSpec((None, D_MODEL, EXPERT_DIM), lambda i, be: (be[i], 0, 0)),
            pl.BlockSpec((None, EXPERT_DIM, D_MODEL), lambda i, be: (be[i], 0, 0)),
        ],
        out_specs=pl.BlockSpec((EXP_BLK, D_MODEL), lambda i, be: (i, 0)),
        scratch_shapes=[pltpu.VMEM((2, EXP_BLK, D_MODEL), F32), pltpu.SemaphoreType.DMA((2,))],
    )
    return pl.pallas_call(
        _expert_kernel,
        grid_spec=grid_spec,
        out_shape=jax.ShapeDtypeStruct((nblk * EXP_BLK, D_MODEL), F32),
        compiler_params=_cparams("arbitrary"),
        name="experts",
    )(blk_e, tok3, tok3, x, wg, wu, wd)


def _combine_kernel(dst_cur_ref, dst_nxt_ref, y_hbm, wts_ref, x_ref, wg_ref, wu_ref, wd_ref,
                    g_ref, b_ref, xo_ref, ybuf, sem):
    i = pl.program_id(0)
    nt = pl.num_programs(0)
    slot = lax.rem(i, 2)
    count = TOP_K * COMB_T

    @pl.when(i == 0)
    def _():
        _gather_start(dst_cur_ref, y_hbm, ybuf, 0, sem, count)

    @pl.when(i + 1 < nt)
    def _():
        _gather_start(dst_nxt_ref, y_hbm, ybuf, 1 - slot, sem, count)

    x = x_ref[...]
    xb = x.astype(BF16)
    shared = _dot((_silu(_dot(xb, wg_ref[...])) * _dot(xb, wu_ref[...])).astype(BF16), wd_ref[...])

    _gather_wait(y_hbm, ybuf, slot, sem, count)
    wts = wts_ref[...]
    routed = None
    for k in range(TOP_K):
        term = ybuf[slot, k * COMB_T:(k + 1) * COMB_T, :] * wts[:, k:k + 1]
        routed = term if routed is None else routed + term
    xo_ref[...] = _layer_norm(DEEPNORM_ALPHA * x + (routed + shared), g_ref[...], b_ref[...])


def _combine(dest, y, wts, x, wg, wu, wd, g, b):
    n = x.shape[0]
    nt = n // COMB_T
    count = TOP_K * COMB_T
    dest3 = dest.reshape(nt, 1, count)
    row = lambda i: (i, 0)
    const = lambda i: (0, 0)
    smem_blk = lambda f: pl.BlockSpec((1, 1, count), f, memory_space=pltpu.SMEM)
    return pl.pallas_call(
        _combine_kernel,
        grid=(nt,),
        in_specs=[
            smem_blk(lambda i: (i, 0, 0)),
            smem_blk(lambda i: (jnp.minimum(i + 1, nt - 1), 0, 0)),
            pl.BlockSpec(memory_space=pl.ANY),
            pl.BlockSpec((COMB_T, TOP_K), row),
            pl.BlockSpec((COMB_T, D_MODEL), row),
            pl.BlockSpec((D_MODEL, EXPERT_DIM), const),
            pl.BlockSpec((D_MODEL, EXPERT_DIM), const),
            pl.BlockSpec((EXPERT_DIM, D_MODEL), const),
            pl.BlockSpec((1, D_MODEL), const),
            pl.BlockSpec((1, D_MODEL), const),
        ],
        out_specs=pl.BlockSpec((COMB_T, D_MODEL), row),
        out_shape=jax.ShapeDtypeStruct((n, D_MODEL), F32),
        scratch_shapes=[pltpu.VMEM((2, count, D_MODEL), F32), pltpu.SemaphoreType.DMA((2,))],
        compiler_params=_cparams("arbitrary"),
        name="combine",
    )(dest3, dest3, y, wts, x, wg, wu, wd, g, b)


def _dispatch_plan(eidx, n):
    nk = n * TOP_K
    nblk = -(-(nk + N_EXPERTS * (EXP_BLK - 1)) // EXP_BLK)
    flat_e = eidx.reshape(-1)
    order = jnp.argsort(flat_e)
    e_sorted = flat_e[order]
    counts = jnp.bincount(flat_e, length=N_EXPERTS)
    padded = (counts + EXP_BLK - 1) // EXP_BLK * EXP_BLK
    start = jnp.cumsum(counts) - counts
    pend = jnp.cumsum(padded)
    pstart = pend - padded
    dest_sorted = (pstart[e_sorted] + (jnp.arange(nk) - start[e_sorted])).astype(jnp.int32)
    tok_pad = jnp.zeros((nblk * EXP_BLK,), jnp.int32).at[dest_sorted].set(
        (order // TOP_K).astype(jnp.int32), unique_indices=True)
    dest = jnp.zeros((nk,), jnp.int32).at[order].set(dest_sorted, unique_indices=True)
    blk_e = jnp.clip(jnp.searchsorted(pend, jnp.arange(nblk) * EXP_BLK, side="right"),
                     0, N_EXPERTS - 1).astype(jnp.int32)
    return blk_e, tok_pad, dest.reshape(n, TOP_K)


def _moe(x, wr_t, bias_col, wg, wu, wd, wgs, wus, wds, g, b):
    n = x.shape[0]
    eidx_t, wts_t = _router(x, wr_t, bias_col)
    eidx = eidx_t.T
    wts = wts_t.T
    blk_e, tok_pad, dest = _dispatch_plan(eidx, n)
    y = _experts(blk_e, tok_pad, x, wg, wu, wd)
    dest_tiles = dest.reshape(n // COMB_T, COMB_T, TOP_K).transpose(0, 2, 1)
    return _combine(dest_tiles, y, wts, x, wgs, wus, wds, g, b)


def kernel(x, w_in, dn_conv_w, dn_a_log, dn_dt_bias, dn_norm_w, w_branch_a, w_branch_b, w_out,
           ln1_g, ln1_b, w_router, router_bias, w_gate_e, w_up_e, w_down_e,
           w_gate_s, w_up_s, w_down_s, ln2_g, ln2_b):
    bsz, seq, d = x.shape
    n = bsz * seq
    assert d == D_MODEL and seq % MOBA_BLOCK == 0 and seq // MOBA_BLOCK <= LANES
    assert n % MM_TM == 0 and seq % PREP_T == 0 and n % MIX_T == 0 and n % ROUTE_T == 0
    depth = w_in.shape[0]
    xf = x.reshape(n, d).astype(F32)
    ab_lo = 7 * D_MODEL
    ab_hi = ab_lo + 2 * N_HEADS
    vec = lambda t: t.astype(F32).reshape(1, D_MODEL)
    for l in range(depth):
        w_main = jnp.concatenate([w_in[l, :, :ab_lo], w_in[l, :, ab_hi:]], axis=1).astype(BF16)
        w_ab = jnp.pad(w_in[l, :, ab_lo:ab_hi], ((0, 0), (0, LANES - 2 * N_HEADS))).astype(BF16)
        xb = xf.astype(BF16)
        proj = _matmul(xb, w_main, BF16, MM_TM, MM_TN)
        ab = _matmul(xb, w_ab, F32, MM_TM, LANES)
        ya = _moba(proj, bsz, seq)
        yb = _gdn(proj, ab, dn_conv_w[l], dn_a_log[l], dn_dt_bias[l], dn_norm_w[l], bsz, seq)
        x1 = _mix(ya, yb, proj, xf, w_branch_a[l].astype(BF16), w_branch_b[l].astype(BF16),
                  w_out[l].astype(BF16), vec(ln1_g[l]), vec(ln1_b[l]))
        xf = _moe(x1, w_router[l].astype(F32).T, router_bias[l].astype(F32).reshape(N_EXPERTS, 1),
                  w_gate_e[l].astype(BF16), w_up_e[l].astype(BF16), w_down_e[l].astype(BF16),
                  w_gate_s[l].astype(BF16), w_up_s[l].astype(BF16), w_down_s[l].astype(BF16),
                  vec(ln2_g[l]), vec(ln2_b[l]))
    return xf.reshape(bsz, seq, d).astype(x.dtype)
```

```python
import functools

import jax
import jax.numpy as jnp
from jax import lax
from jax.experimental import pallas as pl
from jax.experimental.pallas import tpu as pltpu

D_MODEL = 1024
N_HEADS = 8
HEAD_DIM = 128
MOBA_BLOCK = 256
MOBA_TOPK = 3
DN_CONV = 4
DN_CHUNK = 64
N_EXPERTS = 64
N_GROUPS = 8
TOPK_GROUPS = 4
TOP_K = 8
EXPERT_DIM = 256
ROUTED_SCALE = 2.5
LN_EPS = 1e-5
RMS_EPS = 1e-6
L2_EPS = 1e-6
DEPTH = 2
DEEPNORM_ALPHA = (2 * DEPTH) ** 0.25

LANES = 128
SUBLANES = 8
VMEM_LIMIT = 48 * 1024 * 1024

MM_TM = 1024
MM_TN = 1024
GDN_GROUP = 128
INV_BASE = 8
PREP_T = 256
MIX_T = 512
ROUTE_T = 512
EXP_BLK = 256
COMB_T = 128

F32 = jnp.float32
BF16 = jnp.bfloat16
NEG_INF = float("-inf")
HI = lax.Precision.HIGHEST


def _cparams(*sem):
    return pltpu.CompilerParams(dimension_semantics=sem, vmem_limit_bytes=VMEM_LIMIT)


def _dot(a, b):
    return jnp.dot(a, b, preferred_element_type=F32)


def _dot_nt(a, b, precision=None):
    return lax.dot_general(a, b, (((1,), (1,)), ((), ())), precision=precision,
                           preferred_element_type=F32)


def _sigmoid(x):
    return 1.0 / (1.0 + jnp.exp(-x))


def _silu(x):
    return x * _sigmoid(x)


def _mm_kernel(a_ref, b_ref, o_ref):
    o_ref[...] = _dot(a_ref[...], b_ref[...]).astype(o_ref.dtype)


def _matmul(a, b, out_dtype, tm, tn):
    m, k = a.shape
    _, n = b.shape
    return pl.pallas_call(
        _mm_kernel,
        grid=(m // tm, n // tn),
        in_specs=[pl.BlockSpec((tm, k), lambda i, j: (i, 0)),
                  pl.BlockSpec((k, tn), lambda i, j: (0, j))],
        out_specs=pl.BlockSpec((tm, tn), lambda i, j: (i, j)),
        out_shape=jax.ShapeDtypeStruct((m, n), out_dtype),
        compiler_params=_cparams("parallel", "parallel"),
        name="inproj",
    )(a, b)


def _moba_kernel(q_ref, k_ref, v_ref, o_ref, kmean_ref, vt_ref, *, nb):
    j = pl.program_id(2)
    blk = MOBA_BLOCK
    nbp = kmean_ref.shape[0]
    c = (HEAD_DIM ** -0.5) * 1.4426950408889634

    @pl.when(j == 0)
    def _():
        kmean_ref[...] = jnp.zeros_like(kmean_ref)
        eye = (lax.broadcasted_iota(jnp.int32, (HEAD_DIM, HEAD_DIM), 0)
               == lax.broadcasted_iota(jnp.int32, (HEAD_DIM, HEAD_DIM), 1)).astype(BF16)
        for n in range(nb):
            kb = k_ref[n * blk:(n + 1) * blk, :].astype(F32)
            kmean_ref[n:n + 1, :] = jnp.mean(kb, axis=0, keepdims=True)
            vt_ref[n] = _dot_nt(eye, v_ref[n * blk:(n + 1) * blk, :]).astype(BF16)

    q = q_ref[...]
    j0 = pl.multiple_of(j * blk, blk)

    def raw_scores(t):
        a0 = pl.multiple_of(2 * t * blk, blk)
        c0 = pl.multiple_of((2 * t + 1) * blk, blk)
        return _dot_nt(k_ref[pl.ds(a0, blk), :], q), _dot_nt(k_ref[pl.ds(c0, blk), :], q)

    km = kmean_ref[...]
    km_hi = km.astype(BF16)
    km_mid = (km - km_hi.astype(F32)).astype(BF16)
    km_lo = (km - km_hi.astype(F32) - km_mid.astype(F32)).astype(BF16)
    gate = _dot_nt(km_hi, q) + _dot_nt(km_mid, q) + _dot_nt(km_lo, q)
    s_own = _dot_nt(k_ref[pl.ds(j0, blk), :], q)

    row = lax.broadcasted_iota(jnp.int32, gate.shape, 0)
    gate = jnp.where(row < j, gate, NEG_INF)
    picks = []
    for _ in range(MOBA_TOPK):
        m = jnp.max(gate, axis=0, keepdims=True)
        idx = jnp.min(jnp.where(gate == m, row, nbp), axis=0, keepdims=True)
        picks.append(idx)
        gate = jnp.where(row == idx, NEG_INF, gate)

    def masked(t, raw):
        na = 2 * t
        nc = na + 1
        pick_a = (picks[0] == na) | (picks[1] == na) | (picks[2] == na)
        pick_c = ((picks[0] == nc) | (picks[1] == nc) | (picks[2] == nc)) & (nc < j)
        return jnp.where(pick_a, raw[0], NEG_INF), jnp.where(pick_c, raw[1], NEG_INF)

    key_i = lax.broadcasted_iota(jnp.int32, (blk, blk), 0)
    qry_i = lax.broadcasted_iota(jnp.int32, (blk, blk), 1)
    s = jnp.where(key_i <= qry_i, s_own, NEG_INF)
    m0 = jnp.max(s, axis=0, keepdims=True)
    p_own = jnp.exp2((s - m0) * c)
    l0 = jnp.sum(p_own, axis=0, keepdims=True)

    def body(t, carry):
        m_i, l_i, acc_s, pa, pc, ia, ic = carry
        sa, sc = masked(t, raw_scores(t))
        acc = acc_s + (_dot(vt_ref[ia], pa) + _dot(vt_ref[ic], pc))
        m_new = jnp.maximum(m_i, jnp.maximum(jnp.max(sa, axis=0, keepdims=True),
                                             jnp.max(sc, axis=0, keepdims=True)))
        alpha = jnp.exp2((m_i - m_new) * c)
        pa_n = jnp.exp2((sa - m_new) * c)
        pc_n = jnp.exp2((sc - m_new) * c)
        l_new = alpha * l_i + jnp.sum(pa_n, axis=0, keepdims=True) + jnp.sum(pc_n, axis=0, keepdims=True)
        return (m_new, l_new, alpha * acc, pa_n.astype(BF16), pc_n.astype(BF16), 2 * t, 2 * t + 1)

    init = (m0, l0, jnp.zeros((HEAD_DIM, blk), F32), p_own.astype(BF16), jnp.zeros((blk, blk), BF16), j, j)
    _, l_f, acc_s, pa, pc, ia, ic = lax.fori_loop(0, (j + 1) // 2, body, init)
    acc_f = acc_s + (_dot(vt_ref[ia], pa) + _dot(vt_ref[ic], pc))
    o_ref[...] = (acc_f / l_f).T.astype(o_ref.dtype)


def _moba(proj, bsz, seq):
    nb = seq // MOBA_BLOCK
    hpd = D_MODEL // HEAD_DIM
    return pl.pallas_call(
        functools.partial(_moba_kernel, nb=nb),
        grid=(bsz, N_HEADS, nb),
        in_specs=[
            pl.BlockSpec((MOBA_BLOCK, HEAD_DIM), lambda b, h, j: (b * nb + j, h)),
            pl.BlockSpec((seq, HEAD_DIM), lambda b, h, j: (b, hpd + h)),
            pl.BlockSpec((seq, HEAD_DIM), lambda b, h, j: (b, 2 * hpd + h)),
        ],
        out_specs=pl.BlockSpec((MOBA_BLOCK, HEAD_DIM), lambda b, h, j: (b * nb + j, h)),
        out_shape=jax.ShapeDtypeStruct((bsz * seq, D_MODEL), BF16),
        scratch_shapes=[pltpu.VMEM((-(-nb // SUBLANES) * SUBLANES, HEAD_DIM), F32),
                        pltpu.VMEM((nb, HEAD_DIM, MOBA_BLOCK), BF16)],
        compiler_params=_cparams("parallel", "parallel", "arbitrary"),
        name="moba",
    )(proj, proj, proj)


def _gdn_prep_kernel(q_ref, k_ref, v_ref, ab_ref, cw_ref, par_ref,
                     qn_ref, kn_ref, vc_ref, gcb_ref, xbuf_ref, *, tile):
    t = pl.program_id(1)
    halo = SUBLANES

    @pl.when(t == 0)
    def _():
        xbuf_ref[0:halo, :] = jnp.zeros((halo, 3 * D_MODEL), F32)

    @pl.when(t > 0)
    def _():
        xbuf_ref[0:halo, :] = xbuf_ref[tile:tile + halo, :]

    for c, src in enumerate((q_ref, k_ref, v_ref)):
        xbuf_ref[halo:halo + tile, c * D_MODEL:(c + 1) * D_MODEL] = src[...].astype(F32)

    outs = (qn_ref, kn_ref, vc_ref)
    for c in range(3):
        for h in range(N_HEADS):
            lo = c * D_MODEL + h * HEAD_DIM
            acc = None
            for i in range(DN_CONV):
                off = halo - (DN_CONV - 1) + i
                term = xbuf_ref[off:off + tile, lo:lo + HEAD_DIM] * cw_ref[i:i + 1, lo:lo + HEAD_DIM]
                acc = term if acc is None else acc + term
            y = _silu(acc)
            if c < 2:
                y = y * lax.rsqrt(jnp.sum(y * y, axis=-1, keepdims=True) + L2_EPS)
                if c == 0:
                    y = y * (HEAD_DIM ** -0.5)
            outs[c][:, h * HEAD_DIM:(h + 1) * HEAD_DIM] = y.astype(outs[c].dtype)

    ab = ab_ref[...]
    a_log = par_ref[0:1, :]
    dt_bias = par_ref[1:2, :]
    xa = ab + dt_bias
    softplus = jnp.maximum(xa, 0.0) + jnp.log(1.0 + jnp.exp(-jnp.abs(xa)))
    g = -jnp.exp(a_log) * softplus
    ri = lax.broadcasted_iota(jnp.int32, (tile, tile), 0)
    ci = lax.broadcasted_iota(jnp.int32, (tile, tile), 1)
    shift = DN_CHUNK.bit_length() - 1
    tri = ((ci <= ri) & ((ri >> shift) == (ci >> shift))).astype(F32)
    gc = jnp.dot(tri, g, precision=HI, preferred_element_type=F32)
    lane = lax.broadcasted_iota(jnp.int32, ab.shape, 1)
    gcb_ref[...] = jnp.where(lane < N_HEADS, gc, _sigmoid(ab))


def _gdn_prep(proj, ab, conv_w, par, bsz, seq):
    nt = seq // PREP_T
    n = bsz * seq
    row = lambda b, t: (b * nt + t, 0)
    big = pl.BlockSpec((PREP_T, D_MODEL), row)
    return pl.pallas_call(
        functools.partial(_gdn_prep_kernel, tile=PREP_T),
        grid=(bsz, nt),
        in_specs=[
            pl.BlockSpec((PREP_T, D_MODEL), lambda b, t: (b * nt + t, 3)),
            pl.BlockSpec((PREP_T, D_MODEL), lambda b, t: (b * nt + t, 4)),
            pl.BlockSpec((PREP_T, D_MODEL), lambda b, t: (b * nt + t, 5)),
            pl.BlockSpec((PREP_T, LANES), row),
            pl.BlockSpec((DN_CONV, 3 * D_MODEL), lambda b, t: (0, 0)),
            pl.BlockSpec((SUBLANES, LANES), lambda b, t: (0, 0)),
        ],
        out_specs=[big, big, big, pl.BlockSpec((PREP_T, LANES), row)],
        out_shape=[jax.ShapeDtypeStruct((n, D_MODEL), BF16)] * 3
        + [jax.ShapeDtypeStruct((n, LANES), F32)],
        scratch_shapes=[pltpu.VMEM((PREP_T + SUBLANES, 3 * D_MODEL), F32)],
        compiler_params=_cparams("parallel", "arbitrary"),
        name="gdn_prep",
    )(proj, proj, proj, ab, conv_w, par)


def _gdn_wy_kernel(qn_ref, kn_ref, vc_ref, gcb_ref,
                   u_ref, w_ref, qd_ref, kdt_ref, a_ref, gl_ref):
    rows = GDN_GROUP
    gcb = gcb_ref[...]
    gcb_t = gcb.T
    ri = lax.broadcasted_iota(jnp.int32, (rows, rows), 0)
    ci = lax.broadcasted_iota(jnp.int32, (rows, rows), 1)
    shift = DN_CHUNK.bit_length() - 1
    same = (ri >> shift) == (ci >> shift)
    incl = same & (ci <= ri)
    strict = same & (ci < ri)
    eye = ri == ci
    eye_bf = eye.astype(BF16)
    base_shift = INV_BASE.bit_length() - 1
    base_blk = (ri >> base_shift) == (ci >> base_shift)
    merge_masks = []
    for s in range(base_shift, shift):
        merge_masks.append(((ri >> (s + 1)) == (ci >> (s + 1)))
                           & (((ri >> s) & 1) == 1) & (((ci >> s) & 1) == 0))
    rcol = lax.broadcasted_iota(jnp.int32, (rows, 1), 0)
    n_chunks = rows // DN_CHUNK

    heads = range(N_HEADS)
    sls = [slice(h * HEAD_DIM, (h + 1) * HEAD_DIM) for h in heads]
    gcol = [gcb[:, h:h + 1] for h in heads]
    bcol = [gcb[:, N_HEADS + h:N_HEADS + h + 1] for h in heads]
    decay, glrow = [], []
    for h in heads:
        grow = gcb_t[h:h + 1, :]
        decay.append(jnp.where(incl, jnp.exp(jnp.where(incl, gcol[h] - grow, 0.0)), 0.0))
        glast = gcb[DN_CHUNK - 1:DN_CHUNK, h:h + 1]
        gl_ref[n_chunks * h:n_chunks * h + 1, :] = jnp.broadcast_to(glast, (1, LANES))
        glr = jnp.broadcast_to(glast, (rows, 1))
        for c in range(1, n_chunks):
            gl_c = gcb[(c + 1) * DN_CHUNK - 1:(c + 1) * DN_CHUNK, h:h + 1]
            gl_ref[n_chunks * h + c:n_chunks * h + c + 1, :] = jnp.broadcast_to(gl_c, (1, LANES))
            glr = jnp.where(rcol >= c * DN_CHUNK, gl_c, glr)
        glrow.append(glr)

    k = [kn_ref[:, sl].astype(F32) for sl in sls]
    k_bf = [kn_ref[:, sl] for sl in sls]
    kb = [k[h] * bcol[h] for h in heads]
    eg = [jnp.exp(gcol[h]) for h in heads]
    lmat = [jnp.where(strict, _dot_nt(kb[h].astype(BF16), k_bf[h]) * decay[h], 0.0) for h in heads]
    x = [jnp.where(base_blk, -lmat[h], 0.0).astype(BF16) for h in heads]
    tmat = [jnp.where(eye, 1.0, x[h].astype(F32)) for h in heads]
    for _ in range(INV_BASE.bit_length() - 2):
        x = [_dot(x[h], x[h]).astype(BF16) for h in heads]
        tmat = [tmat[h] + _dot(tmat[h].astype(BF16), x[h]) for h in heads]
    for off in merge_masks:
        t_bf = [tmat[h].astype(BF16) for h in heads]
        lower = [_dot(t_bf[h], jnp.where(off, lmat[h], 0.0).astype(BF16)) for h in heads]
        tmat = [tmat[h] - _dot(lower[h].astype(BF16), t_bf[h]) for h in heads]
    for h in heads:
        v = vc_ref[:, sls[h]].astype(F32)
        rhs = jnp.concatenate([v * bcol[h], kb[h] * eg[h]], axis=1).astype(BF16)
        uw = _dot(tmat[h].astype(BF16), rhs)
        u_ref[:, sls[h]] = uw[:, :HEAD_DIM].astype(u_ref.dtype)
        w_ref[:, sls[h]] = uw[:, HEAD_DIM:].astype(w_ref.dtype)
    for h in heads:
        q = qn_ref[:, sls[h]].astype(F32)
        qd_ref[:, sls[h]] = (q * eg[h]).astype(qd_ref.dtype)
        kd = (k[h] * jnp.exp(glrow[h] - gcol[h])).astype(BF16)
        kdt_ref[:, sls[h]] = _dot_nt(eye_bf, kd).astype(kdt_ref.dtype)
        a_ref[:, sls[h]] = jnp.where(incl, _dot_nt(qn_ref[:, sls[h]], k_bf[h]) * decay[h],
                                     0.0).astype(a_ref.dtype)


def _gdn_wy(qn, kn, vc, gcb, bsz, seq):
    ng = seq // GDN_GROUP
    n = bsz * seq
    row = lambda b, g: (b * ng + g, 0)
    big = pl.BlockSpec((GDN_GROUP, D_MODEL), row)
    gl_rows = N_HEADS * (GDN_GROUP // DN_CHUNK)
    return pl.pallas_call(
        _gdn_wy_kernel,
        grid=(bsz, ng),
        in_specs=[big, big, big, pl.BlockSpec((GDN_GROUP, LANES), row)],
        out_specs=[big] * 5 + [pl.BlockSpec((gl_rows, LANES), row)],
        out_shape=[jax.ShapeDtypeStruct((n, D_MODEL), BF16)] * 5
        + [jax.ShapeDtypeStruct((bsz * ng * gl_rows, LANES), F32)],
        compiler_params=_cparams("parallel", "parallel"),
        name="gdn_wy",
    )(qn, kn, vc, gcb)


def _gdn_scan_kernel(u_ref, w_ref, qd_ref, kdt_ref, a_ref, gl_ref, z_ref, nw_ref,
                     y_ref, state_ref):
    g = pl.program_id(1)
    c = DN_CHUNK

    @pl.when(g == 0)
    def _():
        state_ref[...] = jnp.zeros_like(state_ref)

    zeros = jnp.zeros((c, HEAD_DIM), F32)
    nw = nw_ref[...]
    heads = range(N_HEADS)
    sls = [slice(h * HEAD_DIM, (h + 1) * HEAD_DIM) for h in heads]
    s0 = [state_ref[h] for h in heads]
    s0b = [s.astype(BF16) for s in s0]
    vn0 = [u_ref[:c, sls[h]].astype(F32) - _dot(w_ref[:c, sls[h]], s0b[h]) for h in heads]
    oq0 = [_dot(qd_ref[:c, sls[h]], s0b[h]) for h in heads]
    s1 = [s0[h] * jnp.exp(gl_ref[2 * h:2 * h + 1, :])
          + _dot(kdt_ref[:, sls[h]], jnp.concatenate([vn0[h], zeros], axis=0).astype(BF16)) for h in heads]
    s1b = [s.astype(BF16) for s in s1]
    vn1 = [u_ref[c:, sls[h]].astype(F32) - _dot(w_ref[c:, sls[h]], s1b[h]) for h in heads]
    oq1 = [_dot(qd_ref[c:, sls[h]], s1b[h]) for h in heads]
    for h in heads:
        state_ref[h] = (s1[h] * jnp.exp(gl_ref[2 * h + 1:2 * h + 2, :])
                        + _dot(kdt_ref[:, sls[h]], jnp.concatenate([zeros, vn1[h]], axis=0).astype(BF16)))
    for h in heads:
        vn = jnp.concatenate([vn0[h], vn1[h]], axis=0).astype(BF16)
        o = jnp.concatenate([oq0[h], oq1[h]], axis=0) + _dot(a_ref[:, sls[h]], vn)
        ms = jnp.mean(o * o, axis=-1, keepdims=True)
        y = o * lax.rsqrt(ms + RMS_EPS) * nw * _silu(z_ref[:, sls[h]].astype(F32))
        y_ref[:, sls[h]] = y.astype(y_ref.dtype)


def _gdn_scan(u, w, qd, kdt, a, gl, proj, norm_w, bsz, seq):
    assert GDN_GROUP == 2 * DN_CHUNK
    ng = seq // GDN_GROUP
    n = bsz * seq
    row = lambda b, g: (b * ng + g, 0)
    big = pl.BlockSpec((GDN_GROUP, D_MODEL), row)
    gl_rows = N_HEADS * (GDN_GROUP // DN_CHUNK)
    return pl.pallas_call(
        _gdn_scan_kernel,
        grid=(bsz, ng),
        in_specs=[big] * 5 + [
            pl.BlockSpec((gl_rows, LANES), row),
            pl.BlockSpec((GDN_GROUP, D_MODEL), lambda b, g: (b * ng + g, 6)),
            pl.BlockSpec((1, HEAD_DIM), lambda b, g: (0, 0)),
        ],
        out_specs=big,
        out_shape=jax.ShapeDtypeStruct((n, D_MODEL), BF16),
        scratch_shapes=[pltpu.VMEM((N_HEADS, HEAD_DIM, HEAD_DIM), F32)],
        compiler_params=_cparams("parallel", "arbitrary"),
        name="gdn_scan",
    )(u, w, qd, kdt, a, gl, proj, norm_w)


def _gdn(proj, ab, conv_w, a_log, dt_bias, norm_w, bsz, seq):
    par = jnp.zeros((SUBLANES, LANES), F32)
    par = par.at[0, :N_HEADS].set(a_log.astype(F32)).at[1, :N_HEADS].set(dt_bias.astype(F32))
    qn, kn, vc, gcb = _gdn_prep(proj, ab, conv_w.astype(F32), par, bsz, seq)
    u, w, qd, kdt, a, gl = _gdn_wy(qn, kn, vc, gcb, bsz, seq)
    return _gdn_scan(u, w, qd, kdt, a, gl, proj, norm_w.astype(F32).reshape(1, HEAD_DIM), bsz, seq)


def _layer_norm(x, g, b):
    mu = jnp.mean(x, axis=-1, keepdims=True)
    xc = x - mu
    var = jnp.mean(xc * xc, axis=-1, keepdims=True)
    return xc * lax.rsqrt(var + LN_EPS) * g + b


def _mix_kernel(ya_ref, yb_ref, ga_ref, gb_ref, x_ref, wa_ref, wb_ref, wo_ref, g_ref, b_ref,
                xo_ref):
    ma = _dot(ya_ref[...], wa_ref[...])
    mb = _dot(yb_ref[...], wb_ref[...])
    merged = _sigmoid(ga_ref[...].astype(F32)) * ma + _sigmoid(gb_ref[...].astype(F32)) * mb
    mix = _dot(merged.astype(BF16), wo_ref[...])
    xo_ref[...] = _layer_norm(DEEPNORM_ALPHA * x_ref[...] + mix, g_ref[...], b_ref[...])


def _mix(ya, yb, proj, x, wa, wb, wo, g, b):
    n = x.shape[0]
    row = lambda i: (i, 0)
    const = lambda i: (0, 0)
    tile = pl.BlockSpec((MIX_T, D_MODEL), row)
    wspec = pl.BlockSpec((D_MODEL, D_MODEL), const)
    vec = pl.BlockSpec((1, D_MODEL), const)
    return pl.pallas_call(
        _mix_kernel,
        grid=(n // MIX_T,),
        in_specs=[tile, tile,
                  pl.BlockSpec((MIX_T, D_MODEL), lambda i: (i, 7)),
                  pl.BlockSpec((MIX_T, D_MODEL), lambda i: (i, 8)),
                  tile, wspec, wspec, wspec, vec, vec],
        out_specs=tile,
        out_shape=jax.ShapeDtypeStruct((n, D_MODEL), F32),
        compiler_params=_cparams("parallel"),
        name="mix",
    )(ya, yb, proj, proj, x, wa, wb, wo, g, b)


def _first_argmax(vals, iota, size):
    m = jnp.max(vals, axis=0, keepdims=True)
    idx = jnp.min(jnp.where(vals == m, iota, size), axis=0, keepdims=True)
    return m, idx


def _router_kernel(x_ref, wr_ref, bias_ref, eidx_ref, wts_ref):
    tile = x_ref.shape[0]
    gsz = N_EXPERTS // N_GROUPS
    logits = _dot_nt(wr_ref[...], x_ref[...], precision=HI)
    scores = _sigmoid(logits)
    choice = scores + bias_ref[...]
    iota_g = lax.broadcasted_iota(jnp.int32, (gsz, tile), 0)
    iota_e = lax.broadcasted_iota(jnp.int32, (N_EXPERTS, tile), 0)

    gscores = []
    for gi in range(N_GROUPS):
        sub = choice[gi * gsz:(gi + 1) * gsz, :]
        m1, i1 = _first_argmax(sub, iota_g, gsz)
        m2 = jnp.max(jnp.where(iota_g == i1, NEG_INF, sub), axis=0, keepdims=True)
        gscores.append(m1 + m2)
    gs = jnp.concatenate(gscores, axis=0)
    iota_n = lax.broadcasted_iota(jnp.int32, (N_GROUPS, tile), 0)
    gmask = jnp.zeros((N_GROUPS, tile), jnp.bool_)
    for _ in range(TOPK_GROUPS):
        _, gi = _first_argmax(gs, iota_n, N_GROUPS)
        hit = iota_n == gi
        gmask = gmask | hit
        gs = jnp.where(hit, NEG_INF, gs)
    gmask_f = gmask.astype(F32)
    emask = jnp.concatenate(
        [jnp.broadcast_to(gmask_f[gi:gi + 1, :], (gsz, tile)) for gi in range(N_GROUPS)], axis=0) > 0.5

    masked = jnp.where(emask, choice, NEG_INF)
    idxs, ws = [], []
    for _ in range(TOP_K):
        _, ei = _first_argmax(masked, iota_e, N_EXPERTS)
        hit = iota_e == ei
        idxs.append(ei)
        ws.append(jnp.sum(jnp.where(hit, scores, 0.0), axis=0, keepdims=True))
        masked = jnp.where(hit, NEG_INF, masked)
    wsum = ws[0]
    for w in ws[1:]:
        wsum = wsum + w
    eidx_ref[...] = jnp.concatenate(idxs, axis=0)
    wts_ref[...] = jnp.concatenate(ws, axis=0) / wsum * ROUTED_SCALE


def _router(x, wr_t, bias_col):
    n = x.shape[0]
    return pl.pallas_call(
        _router_kernel,
        grid=(n // ROUTE_T,),
        in_specs=[pl.BlockSpec((ROUTE_T, D_MODEL), lambda i: (i, 0)),
                  pl.BlockSpec((N_EXPERTS, D_MODEL), lambda i: (0, 0)),
                  pl.BlockSpec((N_EXPERTS, 1), lambda i: (0, 0))],
        out_specs=[pl.BlockSpec((TOP_K, ROUTE_T), lambda i: (0, i)),
                   pl.BlockSpec((TOP_K, ROUTE_T), lambda i: (0, i))],
        out_shape=[jax.ShapeDtypeStruct((TOP_K, n), jnp.int32),
                   jax.ShapeDtypeStruct((TOP_K, n), F32)],
        compiler_params=_cparams("parallel"),
        name="router",
    )(x, wr_t, bias_col)


CHUNKS = D_MODEL // LANES
RING = 3


def _tile_copy(src_hbm, row, dst_buf, slot, r, sem, count):
    src = src_hbm.at[pl.ds(pl.multiple_of(row * CHUNKS, CHUNKS), CHUNKS)]
    dst = dst_buf.at[pl.ds(pl.multiple_of((slot * count + r) * CHUNKS, CHUNKS), CHUNKS)]
    return pltpu.make_async_copy(src, dst, sem.at[slot])


def _start_tiles(idx_ref, src_hbm, dst_buf, slot, sem, count):
    for r in range(count):
        _tile_copy(src_hbm, idx_ref[0, 0, r], dst_buf, slot, r, sem, count).start()


def _wait_tiles(src_hbm, dst_buf, slot, sem, count):
    def body(r, carry):
        _tile_copy(src_hbm, 0, dst_buf, slot, r, sem, count).wait()
        return carry
    lax.fori_loop(0, count, body, 0, unroll=8)


def _rows_from_tiles(buf, slot, first, count):
    tiles = buf.shape[0] // (RING * CHUNKS)
    base = pl.multiple_of(slot * (tiles * CHUNKS), tiles * CHUNKS) + first * CHUNKS
    return jnp.concatenate([buf[pl.ds(base + c, count, stride=CHUNKS), :] for c in range(CHUNKS)], axis=1)


def _ring_step(i, idx_cur, idx_nxt, idx_ahead, src_hbm, buf, sem, count, compute):
    nsteps = pl.num_programs(0)
    slot = lax.rem(i, RING)

    @pl.when(i == 0)
    def _():
        _start_tiles(idx_cur, src_hbm, buf, 0, sem, count)
        _start_tiles(idx_nxt, src_hbm, buf, 1, sem, count)

    _wait_tiles(src_hbm, buf, slot, sem, count)
    compute(slot)
    _start_tiles(idx_ahead, src_hbm, buf, lax.rem(i + 2, RING), sem, count)

    @pl.when(i == nsteps - 1)
    def _():
        _wait_tiles(src_hbm, buf, lax.rem(i + 1, RING), sem, count)
        _wait_tiles(src_hbm, buf, lax.rem(i + 2, RING), sem, count)


def _ring_specs(nsteps, count, nprefetch):
    def spec(ahead):
        if nprefetch:
            f = lambda i, *_: (jnp.minimum(i + ahead, nsteps - 1), 0, 0)
        else:
            f = lambda i: (jnp.minimum(i + ahead, nsteps - 1), 0, 0)
        return pl.BlockSpec((1, 1, count), f, memory_space=pltpu.SMEM)
    return [spec(0), spec(1), spec(2)]


def _expert_kernel(blk_e_ref, tok_cur, tok_nxt, tok_ahead, x_hbm, wg_ref, wu_ref, wd_ref,
                   y_ref, xbuf, sem):
    del blk_e_ref

    def compute(slot):
        xb = _rows_from_tiles(xbuf, slot, 0, EXP_BLK).astype(BF16)
        hmid = _silu(_dot(xb, wg_ref[...])) * _dot(xb, wu_ref[...])
        y = _dot(hmid.astype(BF16), wd_ref[...])
        for c in range(CHUNKS):
            y_ref[pl.ds(c, EXP_BLK, stride=CHUNKS), :] = y[:, c * LANES:(c + 1) * LANES]

    _ring_step(pl.program_id(0), tok_cur, tok_nxt, tok_ahead, x_hbm, xbuf, sem, EXP_BLK, compute)


def _experts(blk_e, tok_pad, x_tiles, wg, wu, wd):
    nblk = blk_e.shape[0]
    assert nblk >= RING
    tok3 = tok_pad.reshape(nblk, 1, EXP_BLK)
    wmap = lambda i, be: (be[i], 0, 0)
    grid_spec = pltpu.PrefetchScalarGridSpec(
        num_scalar_prefetch=1,
        grid=(nblk,),
        in_specs=_ring_specs(nblk, EXP_BLK, 1) + [
            pl.BlockSpec(memory_space=pl.ANY),
            pl.BlockSpec((None, D_MODEL, EXPERT_DIM), wmap),
            pl.BlockSpec((None, D_MODEL, EXPERT_DIM), wmap),
            pl.BlockSpec((None, EXPERT_DIM, D_MODEL), wmap),
        ],
        out_specs=pl.BlockSpec((EXP_BLK * CHUNKS, LANES), lambda i, be: (i, 0)),
        scratch_shapes=[pltpu.VMEM((RING * EXP_BLK * CHUNKS, LANES), F32), pltpu.SemaphoreType.DMA((RING,))],
    )
    return pl.pallas_call(
        _expert_kernel,
        grid_spec=grid_spec,
        out_shape=jax.ShapeDtypeStruct((nblk * EXP_BLK * CHUNKS, LANES), F32),
        compiler_params=_cparams("arbitrary"),
        name="experts",
    )(blk_e, tok3, tok3, tok3, x_tiles, wg, wu, wd)


def _combine_kernel(dst_cur, dst_nxt, dst_ahead, y_hbm, wts_ref, x_ref, wg_ref, wu_ref, wd_ref,
                    g_ref, b_ref, xo_ref, ybuf, sem):
    def compute(slot):
        x = x_ref[...]
        xb = x.astype(BF16)
        shared = _dot((_silu(_dot(xb, wg_ref[...])) * _dot(xb, wu_ref[...])).astype(BF16), wd_ref[...])
        wts = wts_ref[...]
        routed = None
        for k in range(TOP_K):
            term = _rows_from_tiles(ybuf, slot, k * COMB_T, COMB_T) * wts[:, k:k + 1]
            routed = term if routed is None else routed + term
        xo_ref[...] = _layer_norm(DEEPNORM_ALPHA * x + (routed + shared), g_ref[...], b_ref[...])

    _ring_step(pl.program_id(0), dst_cur, dst_nxt, dst_ahead, y_hbm, ybuf, sem, TOP_K * COMB_T, compute)


def _combine(dest, y_tiles, wts, x, wg, wu, wd, g, b):
    n = x.shape[0]
    nt = n // COMB_T
    assert nt >= RING
    count = TOP_K * COMB_T
    dest3 = dest.reshape(nt, 1, count)
    row = lambda i: (i, 0)
    const = lambda i: (0, 0)
    return pl.pallas_call(
        _combine_kernel,
        grid=(nt,),
        in_specs=_ring_specs(nt, count, 0) + [
            pl.BlockSpec(memory_space=pl.ANY),
            pl.BlockSpec((COMB_T, TOP_K), row),
            pl.BlockSpec((COMB_T, D_MODEL), row),
            pl.BlockSpec((D_MODEL, EXPERT_DIM), const),
            pl.BlockSpec((D_MODEL, EXPERT_DIM), const),
            pl.BlockSpec((EXPERT_DIM, D_MODEL), const),
            pl.BlockSpec((1, D_MODEL), const),
            pl.BlockSpec((1, D_MODEL), const),
        ],
        out_specs=pl.BlockSpec((COMB_T, D_MODEL), row),
        out_shape=jax.ShapeDtypeStruct((n, D_MODEL), F32),
        scratch_shapes=[pltpu.VMEM((RING * count * CHUNKS, LANES), F32), pltpu.SemaphoreType.DMA((RING,))],
        compiler_params=_cparams("arbitrary"),
        name="combine",
    )(dest3, dest3, dest3, y_tiles, wts, x, wg, wu, wd, g, b)


def _dispatch_plan(eidx, n):
    nk = n * TOP_K
    nblk = -(-(nk + N_EXPERTS * (EXP_BLK - 1)) // EXP_BLK)
    i32 = jnp.int32
    experts = jnp.arange(N_EXPERTS, dtype=i32)
    flat_e = eidx.reshape(-1).astype(i32)
    e_sorted, order = lax.sort_key_val(flat_e, jnp.arange(nk, dtype=i32))
    onehot_sorted = e_sorted[:, None] == experts[None, :]
    counts = jnp.sum(onehot_sorted, axis=0, dtype=i32)
    padded = (counts + EXP_BLK - 1) // EXP_BLK * EXP_BLK
    start = jnp.cumsum(counts) - counts
    pend = jnp.cumsum(padded)
    pstart = pend - padded
    shift = jnp.sum(jnp.where(onehot_sorted, (pstart - start)[None, :], 0), axis=1, dtype=i32)
    dest_sorted = jnp.arange(nk, dtype=i32) + shift
    _, dest = lax.sort_key_val(order, dest_sorted)
    blk_start = jnp.arange(nblk, dtype=i32) * EXP_BLK
    blk_e = jnp.minimum(jnp.sum(blk_start[:, None] >= pend[None, :], axis=1, dtype=i32), N_EXPERTS - 1)
    blk_onehot = blk_e[:, None] == experts[None, :]
    pick = lambda table: jnp.sum(jnp.where(blk_onehot, table[None, :], 0), axis=1, dtype=i32)
    q = (blk_start - pick(pstart))[:, None] + jnp.arange(EXP_BLK, dtype=i32)[None, :]
    valid = (q >= 0) & (q < pick(counts)[:, None])
    src = jnp.clip(pick(start)[:, None] + q, 0, nk - 1)
    tok_pad = jnp.where(valid, order[src] // TOP_K, 0).astype(i32).reshape(-1)
    return blk_e, tok_pad, dest.reshape(n, TOP_K)


def _moe(x, wr_t, bias_col, wg, wu, wd, wgs, wus, wds, g, b):
    n = x.shape[0]
    eidx_t, wts_t = _router(x, wr_t, bias_col)
    eidx = eidx_t.T
    wts = wts_t.T
    blk_e, tok_pad, dest = _dispatch_plan(eidx, n)
    y = _experts(blk_e, tok_pad, x.reshape(n * CHUNKS, LANES), wg, wu, wd)
    dest_tiles = dest.reshape(n // COMB_T, COMB_T, TOP_K).transpose(0, 2, 1)
    return _combine(dest_tiles, y, wts, x, wgs, wus, wds, g, b)


def kernel(x, w_in, dn_conv_w, dn_a_log, dn_dt_bias, dn_norm_w, w_branch_a, w_branch_b, w_out,
           ln1_g, ln1_b, w_router, router_bias, w_gate_e, w_up_e, w_down_e,
           w_gate_s, w_up_s, w_down_s, ln2_g, ln2_b):
    bsz, seq, d = x.shape
    n = bsz * seq
    assert d == D_MODEL and seq % (2 * MOBA_BLOCK) == 0
    assert n % MM_TM == 0 and seq % PREP_T == 0 and n % MIX_T == 0 and n % ROUTE_T == 0
    depth = w_in.shape[0]
    xf = x.reshape(n, d).astype(F32)
    ab_lo = 7 * D_MODEL
    ab_hi = ab_lo + 2 * N_HEADS
    vec = lambda t: t.astype(F32).reshape(1, D_MODEL)
    for l in range(depth):
        w_main = jnp.concatenate([w_in[l, :, :ab_lo], w_in[l, :, ab_hi:]], axis=1).astype(BF16)
        w_ab = jnp.pad(w_in[l, :, ab_lo:ab_hi], ((0, 0), (0, LANES - 2 * N_HEADS))).astype(BF16)
        xb = xf.astype(BF16)
        proj = _matmul(xb, w_main, BF16, MM_TM, MM_TN)
        ab = _matmul(xb, w_ab, F32, MM_TM, LANES)
        ya = _moba(proj, bsz, seq)
        yb = _gdn(proj, ab, dn_conv_w[l], dn_a_log[l], dn_dt_bias[l], dn_norm_w[l], bsz, seq)
        x1 = _mix(ya, yb, proj, xf, w_branch_a[l].astype(BF16), w_branch_b[l].astype(BF16),
                  w_out[l].astype(BF16), vec(ln1_g[l]), vec(ln1_b[l]))
        xf = _moe(x1, w_router[l].astype(F32).T, router_bias[l].astype(F32).reshape(N_EXPERTS, 1),
                  w_gate_e[l].astype(BF16), w_up_e[l].astype(BF16), w_down_e[l].astype(BF16),
                  w_gate_s[l].astype(BF16), w_up_s[l].astype(BF16), w_down_s[l].astype(BF16),
                  vec(ln2_g[l]), vec(ln2_b[l]))
    return xf.reshape(bsz, seq, d).astype(x.dtype)
```

```python
import functools

import jax
import jax.numpy as jnp
from jax import lax
from jax.experimental import pallas as pl
from jax.experimental.pallas import tpu as pltpu

D_MODEL = 1024
N_HEADS = 8
HEAD_DIM = 128
MOBA_BLOCK = 256
MOBA_TOPK = 3
DN_CONV = 4
DN_CHUNK = 64
N_EXPERTS = 64
N_GROUPS = 8
TOPK_GROUPS = 4
TOP_K = 8
EXPERT_DIM = 256
ROUTED_SCALE = 2.5
LN_EPS = 1e-5
RMS_EPS = 1e-6
L2_EPS = 1e-6
DEPTH = 2
DEEPNORM_ALPHA = (2 * DEPTH) ** 0.25

LANES = 128
SUBLANES = 8
VMEM_LIMIT = 48 * 1024 * 1024

MM_TM = 1024
MM_TN = 1024
MOBA_GROUP = 2
GDN_GROUP = 128
INV_BASE = 8
PREP_T = 256
MIX_T = 512
ROUTE_T = 512
EXP_BLK = 256
COMB_T = 128

F32 = jnp.float32
BF16 = jnp.bfloat16
NEG_INF = float("-inf")
HI = lax.Precision.HIGHEST


def _cparams(*sem):
    return pltpu.CompilerParams(dimension_semantics=sem, vmem_limit_bytes=VMEM_LIMIT)


def _dot(a, b):
    return jnp.dot(a, b, preferred_element_type=F32)


def _dot_nt(a, b, precision=None):
    return lax.dot_general(a, b, (((1,), (1,)), ((), ())), precision=precision,
                           preferred_element_type=F32)


def _sigmoid(x):
    return 1.0 / (1.0 + jnp.exp(-x))


def _silu(x):
    return x * _sigmoid(x)


def _mm_kernel(a_ref, b_ref, o_ref):
    o_ref[...] = _dot(a_ref[...], b_ref[...]).astype(o_ref.dtype)


def _matmul(a, b, out_dtype, tm, tn):
    m, k = a.shape
    _, n = b.shape
    return pl.pallas_call(
        _mm_kernel,
        grid=(m // tm, n // tn),
        in_specs=[pl.BlockSpec((tm, k), lambda i, j: (i, 0)),
                  pl.BlockSpec((k, tn), lambda i, j: (0, j))],
        out_specs=pl.BlockSpec((tm, tn), lambda i, j: (i, j)),
        out_shape=jax.ShapeDtypeStruct((m, n), out_dtype),
        compiler_params=_cparams("parallel", "parallel"),
        name="inproj",
    )(a, b)


def _moba_kernel(q_ref, k_ref, v_ref, o_ref, kmean_ref, vt_ref, *, nb):
    j = pl.program_id(2)
    blk = MOBA_BLOCK
    nbp = kmean_ref.shape[1]
    heads = range(MOBA_GROUP)
    sls = [slice(h * HEAD_DIM, (h + 1) * HEAD_DIM) for h in heads]
    c = (HEAD_DIM ** -0.5) * 1.4426950408889634

    @pl.when(j == 0)
    def _():
        kmean_ref[...] = jnp.zeros_like(kmean_ref)
        eye = (lax.broadcasted_iota(jnp.int32, (HEAD_DIM, HEAD_DIM), 0)
               == lax.broadcasted_iota(jnp.int32, (HEAD_DIM, HEAD_DIM), 1)).astype(BF16)
        for h in heads:
            for n in range(nb):
                kb = k_ref[n * blk:(n + 1) * blk, sls[h]].astype(F32)
                kmean_ref[h, n:n + 1, :] = jnp.mean(kb, axis=0, keepdims=True)
                vt_ref[h, n] = _dot_nt(eye, v_ref[n * blk:(n + 1) * blk, sls[h]]).astype(BF16)

    q = [q_ref[:, sl] for sl in sls]
    j0 = pl.multiple_of(j * blk, blk)

    def raw_scores(t):
        a0 = pl.multiple_of(2 * t * blk, blk)
        c0 = pl.multiple_of((2 * t + 1) * blk, blk)
        return [(_dot_nt(k_ref[pl.ds(a0, blk), sls[h]], q[h]),
                 _dot_nt(k_ref[pl.ds(c0, blk), sls[h]], q[h])) for h in heads]

    def values(ia, ic, pa, pc):
        return [_dot(vt_ref[h, ia], pa[h]) + _dot(vt_ref[h, ic], pc[h]) for h in heads]

    gate = []
    for h in heads:
        km = kmean_ref[h]
        km_hi = km.astype(BF16)
        km_mid = (km - km_hi.astype(F32)).astype(BF16)
        km_lo = (km - km_hi.astype(F32) - km_mid.astype(F32)).astype(BF16)
        gate.append(_dot_nt(km_hi, q[h]) + _dot_nt(km_mid, q[h]) + _dot_nt(km_lo, q[h]))
    s_own = [_dot_nt(k_ref[pl.ds(j0, blk), sls[h]], q[h]) for h in heads]

    row = lax.broadcasted_iota(jnp.int32, (nbp, blk), 0)
    picks = []
    for h in heads:
        g = jnp.where(row < j, gate[h], NEG_INF)
        mine = []
        for _ in range(MOBA_TOPK):
            m = jnp.max(g, axis=0, keepdims=True)
            idx = jnp.min(jnp.where(g == m, row, nbp), axis=0, keepdims=True)
            mine.append(idx)
            g = jnp.where(row == idx, NEG_INF, g)
        picks.append(mine)

    def masked(t, raw, h):
        na = 2 * t
        nc = na + 1
        pk = picks[h]
        pick_a = (pk[0] == na) | (pk[1] == na) | (pk[2] == na)
        pick_c = ((pk[0] == nc) | (pk[1] == nc) | (pk[2] == nc)) & (nc < j)
        return jnp.where(pick_a, raw[0], NEG_INF), jnp.where(pick_c, raw[1], NEG_INF)

    key_i = lax.broadcasted_iota(jnp.int32, (blk, blk), 0)
    qry_i = lax.broadcasted_iota(jnp.int32, (blk, blk), 1)
    m0, l0, p_own = [], [], []
    for h in heads:
        s = jnp.where(key_i <= qry_i, s_own[h], NEG_INF)
        m = jnp.max(s, axis=0, keepdims=True)
        p = jnp.exp2((s - m) * c)
        m0.append(m)
        l0.append(jnp.sum(p, axis=0, keepdims=True))
        p_own.append(p.astype(BF16))

    def body(t, carry):
        ia, ic, state = carry
        raw = raw_scores(t)
        pv = values(ia, ic, [s[3] for s in state], [s[4] for s in state])
        new_state = []
        for h in heads:
            m_i, l_i, acc_s, _, _ = state[h]
            sa, sc = masked(t, raw[h], h)
            acc = acc_s + pv[h]
            m_new = jnp.maximum(m_i, jnp.maximum(jnp.max(sa, axis=0, keepdims=True),
                                                 jnp.max(sc, axis=0, keepdims=True)))
            alpha = jnp.exp2((m_i - m_new) * c)
            pa_n = jnp.exp2((sa - m_new) * c)
            pc_n = jnp.exp2((sc - m_new) * c)
            l_new = (alpha * l_i + jnp.sum(pa_n, axis=0, keepdims=True)
                     + jnp.sum(pc_n, axis=0, keepdims=True))
            new_state.append((m_new, l_new, alpha * acc, pa_n.astype(BF16), pc_n.astype(BF16)))
        return 2 * t, 2 * t + 1, tuple(new_state)

    init = tuple((m0[h], l0[h], jnp.zeros((HEAD_DIM, blk), F32), p_own[h], jnp.zeros((blk, blk), BF16))
                 for h in heads)
    ia, ic, state = lax.fori_loop(0, (j + 1) // 2, body, (j, j, init))
    pv = values(ia, ic, [s[3] for s in state], [s[4] for s in state])
    for h in heads:
        o_ref[:, sls[h]] = ((state[h][2] + pv[h]) / state[h][1]).T.astype(o_ref.dtype)


def _moba(proj, bsz, seq):
    nb = seq // MOBA_BLOCK
    width = MOBA_GROUP * HEAD_DIM
    groups = N_HEADS // MOBA_GROUP
    return pl.pallas_call(
        functools.partial(_moba_kernel, nb=nb),
        grid=(bsz, groups, nb),
        in_specs=[
            pl.BlockSpec((MOBA_BLOCK, width), lambda b, h, j: (b * nb + j, h)),
            pl.BlockSpec((seq, width), lambda b, h, j: (b, groups + h)),
            pl.BlockSpec((seq, width), lambda b, h, j: (b, 2 * groups + h)),
        ],
        out_specs=pl.BlockSpec((MOBA_BLOCK, width), lambda b, h, j: (b * nb + j, h)),
        out_shape=jax.ShapeDtypeStruct((bsz * seq, D_MODEL), BF16),
        scratch_shapes=[pltpu.VMEM((MOBA_GROUP, -(-nb // SUBLANES) * SUBLANES, HEAD_DIM), F32),
                        pltpu.VMEM((MOBA_GROUP, nb, HEAD_DIM, MOBA_BLOCK), BF16)],
        compiler_params=_cparams("parallel", "parallel", "arbitrary"),
        name="moba",
    )(proj, proj, proj)


def _gdn_prep_kernel(q_ref, k_ref, v_ref, ab_ref, cw_ref, par_ref,
                     qn_ref, kn_ref, vc_ref, gcb_ref, xbuf_ref, *, tile):
    t = pl.program_id(1)
    halo = SUBLANES

    @pl.when(t == 0)
    def _():
        xbuf_ref[0:halo, :] = jnp.zeros((halo, 3 * D_MODEL), F32)

    @pl.when(t > 0)
    def _():
        xbuf_ref[0:halo, :] = xbuf_ref[tile:tile + halo, :]

    for c, src in enumerate((q_ref, k_ref, v_ref)):
        xbuf_ref[halo:halo + tile, c * D_MODEL:(c + 1) * D_MODEL] = src[...].astype(F32)

    outs = (qn_ref, kn_ref, vc_ref)
    for c in range(3):
        for h in range(N_HEADS):
            lo = c * D_MODEL + h * HEAD_DIM
            acc = None
            for i in range(DN_CONV):
                off = halo - (DN_CONV - 1) + i
                term = xbuf_ref[off:off + tile, lo:lo + HEAD_DIM] * cw_ref[i:i + 1, lo:lo + HEAD_DIM]
                acc = term if acc is None else acc + term
            y = _silu(acc)
            if c < 2:
                y = y * lax.rsqrt(jnp.sum(y * y, axis=-1, keepdims=True) + L2_EPS)
                if c == 0:
                    y = y * (HEAD_DIM ** -0.5)
            outs[c][:, h * HEAD_DIM:(h + 1) * HEAD_DIM] = y.astype(outs[c].dtype)

    ab = ab_ref[...]
    a_log = par_ref[0:1, :]
    dt_bias = par_ref[1:2, :]
    xa = ab + dt_bias
    softplus = jnp.maximum(xa, 0.0) + jnp.log(1.0 + jnp.exp(-jnp.abs(xa)))
    g = -jnp.exp(a_log) * softplus
    ri = lax.broadcasted_iota(jnp.int32, (tile, tile), 0)
    ci = lax.broadcasted_iota(jnp.int32, (tile, tile), 1)
    shift = DN_CHUNK.bit_length() - 1
    tri = ((ci <= ri) & ((ri >> shift) == (ci >> shift))).astype(F32)
    gc = jnp.dot(tri, g, precision=HI, preferred_element_type=F32)
    lane = lax.broadcasted_iota(jnp.int32, ab.shape, 1)
    gcb_ref[...] = jnp.where(lane < N_HEADS, gc, _sigmoid(ab))


def _gdn_prep(proj, ab, conv_w, par, bsz, seq):
    nt = seq // PREP_T
    n = bsz * seq
    row = lambda b, t: (b * nt + t, 0)
    big = pl.BlockSpec((PREP_T, D_MODEL), row)
    return pl.pallas_call(
        functools.partial(_gdn_prep_kernel, tile=PREP_T),
        grid=(bsz, nt),
        in_specs=[
            pl.BlockSpec((PREP_T, D_MODEL), lambda b, t: (b * nt + t, 3)),
            pl.BlockSpec((PREP_T, D_MODEL), lambda b, t: (b * nt + t, 4)),
            pl.BlockSpec((PREP_T, D_MODEL), lambda b, t: (b * nt + t, 5)),
            pl.BlockSpec((PREP_T, LANES), row),
            pl.BlockSpec((DN_CONV, 3 * D_MODEL), lambda b, t: (0, 0)),
            pl.BlockSpec((SUBLANES, LANES), lambda b, t: (0, 0)),
        ],
        out_specs=[big, big, big, pl.BlockSpec((PREP_T, LANES), row)],
        out_shape=[jax.ShapeDtypeStruct((n, D_MODEL), BF16)] * 3
        + [jax.ShapeDtypeStruct((n, LANES), F32)],
        scratch_shapes=[pltpu.VMEM((PREP_T + SUBLANES, 3 * D_MODEL), F32)],
        compiler_params=_cparams("parallel", "arbitrary"),
        name="gdn_prep",
    )(proj, proj, proj, ab, conv_w, par)


def _gdn_wy_kernel(qn_ref, kn_ref, vc_ref, gcb_ref,
                   u_ref, w_ref, qd_ref, kdt_ref, a_ref, gl_ref):
    rows = GDN_GROUP
    gcb = gcb_ref[...]
    gcb_t = gcb.T
    ri = lax.broadcasted_iota(jnp.int32, (rows, rows), 0)
    ci = lax.broadcasted_iota(jnp.int32, (rows, rows), 1)
    shift = DN_CHUNK.bit_length() - 1
    same = (ri >> shift) == (ci >> shift)
    incl = same & (ci <= ri)
    strict = same & (ci < ri)
    eye = ri == ci
    eye_bf = eye.astype(BF16)
    base_shift = INV_BASE.bit_length() - 1
    base_blk = (ri >> base_shift) == (ci >> base_shift)
    merge_masks = []
    for s in range(base_shift, shift):
        merge_masks.append(((ri >> (s + 1)) == (ci >> (s + 1)))
                           & (((ri >> s) & 1) == 1) & (((ci >> s) & 1) == 0))
    rcol = lax.broadcasted_iota(jnp.int32, (rows, 1), 0)
    n_chunks = rows // DN_CHUNK

    heads = range(N_HEADS)
    sls = [slice(h * HEAD_DIM, (h + 1) * HEAD_DIM) for h in heads]
    gcol = [gcb[:, h:h + 1] for h in heads]
    bcol = [gcb[:, N_HEADS + h:N_HEADS + h + 1] for h in heads]
    decay, glrow = [], []
    for h in heads:
        grow = gcb_t[h:h + 1, :]
        decay.append(jnp.where(incl, jnp.exp(jnp.where(incl, gcol[h] - grow, 0.0)), 0.0))
        glast = gcb[DN_CHUNK - 1:DN_CHUNK, h:h + 1]
        gl_ref[n_chunks * h:n_chunks * h + 1, :] = jnp.broadcast_to(glast, (1, LANES))
        glr = jnp.broadcast_to(glast, (rows, 1))
        for c in range(1, n_chunks):
            gl_c = gcb[(c + 1) * DN_CHUNK - 1:(c + 1) * DN_CHUNK, h:h + 1]
            gl_ref[n_chunks * h + c:n_chunks * h + c + 1, :] = jnp.broadcast_to(gl_c, (1, LANES))
            glr = jnp.where(rcol >= c * DN_CHUNK, gl_c, glr)
        glrow.append(glr)

    k = [kn_ref[:, sl].astype(F32) for sl in sls]
    k_bf = [kn_ref[:, sl] for sl in sls]
    kb = [k[h] * bcol[h] for h in heads]
    eg = [jnp.exp(gcol[h]) for h in heads]
    lmat = [jnp.where(strict, _dot_nt(kb[h].astype(BF16), k_bf[h]) * decay[h], 0.0) for h in heads]
    x = [jnp.where(base_blk, -lmat[h], 0.0).astype(BF16) for h in heads]
    tmat = [jnp.where(eye, 1.0, x[h].astype(F32)) for h in heads]
    for _ in range(INV_BASE.bit_length() - 2):
        x = [_dot(x[h], x[h]).astype(BF16) for h in heads]
        tmat = [tmat[h] + _dot(tmat[h].astype(BF16), x[h]) for h in heads]
    for off in merge_masks:
        t_bf = [tmat[h].astype(BF16) for h in heads]
        lower = [_dot(t_bf[h], jnp.where(off, lmat[h], 0.0).astype(BF16)) for h in heads]
        tmat = [tmat[h] - _dot(lower[h].astype(BF16), t_bf[h]) for h in heads]
    for h in heads:
        v = vc_ref[:, sls[h]].astype(F32)
        rhs = jnp.concatenate([v * bcol[h], kb[h] * eg[h]], axis=1).astype(BF16)
        uw = _dot(tmat[h].astype(BF16), rhs)
        u_ref[:, sls[h]] = uw[:, :HEAD_DIM].astype(u_ref.dtype)
        w_ref[:, sls[h]] = uw[:, HEAD_DIM:].astype(w_ref.dtype)
    for h in heads:
        q = qn_ref[:, sls[h]].astype(F32)
        qd_ref[:, sls[h]] = (q * eg[h]).astype(qd_ref.dtype)
        kd = (k[h] * jnp.exp(glrow[h] - gcol[h])).astype(BF16)
        kdt_ref[:, sls[h]] = _dot_nt(eye_bf, kd).astype(kdt_ref.dtype)
        a_ref[:, sls[h]] = jnp.where(incl, _dot_nt(qn_ref[:, sls[h]], k_bf[h]) * decay[h],
                                     0.0).astype(a_ref.dtype)


def _gdn_wy(qn, kn, vc, gcb, bsz, seq):
    ng = seq // GDN_GROUP
    n = bsz * seq
    row = lambda b, g: (b * ng + g, 0)
    big = pl.BlockSpec((GDN_GROUP, D_MODEL), row)
    gl_rows = N_HEADS * (GDN_GROUP // DN_CHUNK)
    return pl.pallas_call(
        _gdn_wy_kernel,
        grid=(bsz, ng),
        in_specs=[big, big, big, pl.BlockSpec((GDN_GROUP, LANES), row)],
        out_specs=[big] * 5 + [pl.BlockSpec((gl_rows, LANES), row)],
        out_shape=[jax.ShapeDtypeStruct((n, D_MODEL), BF16)] * 5
        + [jax.ShapeDtypeStruct((bsz * ng * gl_rows, LANES), F32)],
        compiler_params=_cparams("parallel", "parallel"),
        name="gdn_wy",
    )(qn, kn, vc, gcb)


def _gdn_scan_kernel(u_ref, w_ref, qd_ref, kdt_ref, a_ref, gl_ref, z_ref, nw_ref,
                     y_ref, state_ref):
    g = pl.program_id(1)
    c = DN_CHUNK

    @pl.when(g == 0)
    def _():
        state_ref[...] = jnp.zeros_like(state_ref)

    zeros = jnp.zeros((c, HEAD_DIM), F32)
    nw = nw_ref[...]
    heads = range(N_HEADS)
    sls = [slice(h * HEAD_DIM, (h + 1) * HEAD_DIM) for h in heads]
    s0 = [state_ref[h] for h in heads]
    s0b = [s.astype(BF16) for s in s0]
    vn0 = [u_ref[:c, sls[h]].astype(F32) - _dot(w_ref[:c, sls[h]], s0b[h]) for h in heads]
    oq0 = [_dot(qd_ref[:c, sls[h]], s0b[h]) for h in heads]
    s1 = [s0[h] * jnp.exp(gl_ref[2 * h:2 * h + 1, :])
          + _dot(kdt_ref[:, sls[h]], jnp.concatenate([vn0[h], zeros], axis=0).astype(BF16)) for h in heads]
    s1b = [s.astype(BF16) for s in s1]
    vn1 = [u_ref[c:, sls[h]].astype(F32) - _dot(w_ref[c:, sls[h]], s1b[h]) for h in heads]
    oq1 = [_dot(qd_ref[c:, sls[h]], s1b[h]) for h in heads]
    for h in heads:
        state_ref[h] = (s1[h] * jnp.exp(gl_ref[2 * h + 1:2 * h + 2, :])
                        + _dot(kdt_ref[:, sls[h]], jnp.concatenate([zeros, vn1[h]], axis=0).astype(BF16)))
    for h in heads:
        vn = jnp.concatenate([vn0[h], vn1[h]], axis=0).astype(BF16)
        o = jnp.concatenate([oq0[h], oq1[h]], axis=0) + _dot(a_ref[:, sls[h]], vn)
        ms = jnp.mean(o * o, axis=-1, keepdims=True)
        y = o * lax.rsqrt(ms + RMS_EPS) * nw * _silu(z_ref[:, sls[h]].astype(F32))
        y_ref[:, sls[h]] = y.astype(y_ref.dtype)


def _gdn_scan(u, w, qd, kdt, a, gl, proj, norm_w, bsz, seq):
    assert GDN_GROUP == 2 * DN_CHUNK
    ng = seq // GDN_GROUP
    n = bsz * seq
    row = lambda b, g: (b * ng + g, 0)
    big = pl.BlockSpec((GDN_GROUP, D_MODEL), row)
    gl_rows = N_HEADS * (GDN_GROUP // DN_CHUNK)
    return pl.pallas_call(
        _gdn_scan_kernel,
        grid=(bsz, ng),
        in_specs=[big] * 5 + [
            pl.BlockSpec((gl_rows, LANES), row),
            pl.BlockSpec((GDN_GROUP, D_MODEL), lambda b, g: (b * ng + g, 6)),
            pl.BlockSpec((1, HEAD_DIM), lambda b, g: (0, 0)),
        ],
        out_specs=big,
        out_shape=jax.ShapeDtypeStruct((n, D_MODEL), BF16),
        scratch_shapes=[pltpu.VMEM((N_HEADS, HEAD_DIM, HEAD_DIM), F32)],
        compiler_params=_cparams("parallel", "arbitrary"),
        name="gdn_scan",
    )(u, w, qd, kdt, a, gl, proj, norm_w)


def _gdn(proj, ab, conv_w, a_log, dt_bias, norm_w, bsz, seq):
    par = jnp.zeros((SUBLANES, LANES), F32)
    par = par.at[0, :N_HEADS].set(a_log.astype(F32)).at[1, :N_HEADS].set(dt_bias.astype(F32))
    qn, kn, vc, gcb = _gdn_prep(proj, ab, conv_w.astype(F32), par, bsz, seq)
    u, w, qd, kdt, a, gl = _gdn_wy(qn, kn, vc, gcb, bsz, seq)
    return _gdn_scan(u, w, qd, kdt, a, gl, proj, norm_w.astype(F32).reshape(1, HEAD_DIM), bsz, seq)


def _layer_norm(x, g, b):
    mu = jnp.mean(x, axis=-1, keepdims=True)
    xc = x - mu
    var = jnp.mean(xc * xc, axis=-1, keepdims=True)
    return xc * lax.rsqrt(var + LN_EPS) * g + b


def _mix_kernel(ya_ref, yb_ref, ga_ref, gb_ref, x_ref, wa_ref, wb_ref, wo_ref, g_ref, b_ref,
                xo_ref):
    ma = _dot(ya_ref[...], wa_ref[...])
    mb = _dot(yb_ref[...], wb_ref[...])
    merged = _sigmoid(ga_ref[...].astype(F32)) * ma + _sigmoid(gb_ref[...].astype(F32)) * mb
    mix = _dot(merged.astype(BF16), wo_ref[...])
    xo_ref[...] = _layer_norm(DEEPNORM_ALPHA * x_ref[...] + mix, g_ref[...], b_ref[...])


def _mix(ya, yb, proj, x, wa, wb, wo, g, b):
    n = x.shape[0]
    row = lambda i: (i, 0)
    const = lambda i: (0, 0)
    tile = pl.BlockSpec((MIX_T, D_MODEL), row)
    wspec = pl.BlockSpec((D_MODEL, D_MODEL), const)
    vec = pl.BlockSpec((1, D_MODEL), const)
    return pl.pallas_call(
        _mix_kernel,
        grid=(n // MIX_T,),
        in_specs=[tile, tile,
                  pl.BlockSpec((MIX_T, D_MODEL), lambda i: (i, 7)),
                  pl.BlockSpec((MIX_T, D_MODEL), lambda i: (i, 8)),
                  tile, wspec, wspec, wspec, vec, vec],
        out_specs=tile,
        out_shape=jax.ShapeDtypeStruct((n, D_MODEL), F32),
        compiler_params=_cparams("parallel"),
        name="mix",
    )(ya, yb, proj, proj, x, wa, wb, wo, g, b)


def _first_argmax(vals, iota, size):
    m = jnp.max(vals, axis=0, keepdims=True)
    idx = jnp.min(jnp.where(vals == m, iota, size), axis=0, keepdims=True)
    return m, idx


def _router_kernel(x_ref, wr_ref, bias_ref, eidx_ref, wts_ref):
    tile = x_ref.shape[0]
    gsz = N_EXPERTS // N_GROUPS
    logits = _dot_nt(wr_ref[...], x_ref[...], precision=HI)
    scores = _sigmoid(logits)
    choice = scores + bias_ref[...]
    iota_g = lax.broadcasted_iota(jnp.int32, (gsz, tile), 0)
    iota_e = lax.broadcasted_iota(jnp.int32, (N_EXPERTS, tile), 0)

    gscores = []
    for gi in range(N_GROUPS):
        sub = choice[gi * gsz:(gi + 1) * gsz, :]
        m1, i1 = _first_argmax(sub, iota_g, gsz)
        m2 = jnp.max(jnp.where(iota_g == i1, NEG_INF, sub), axis=0, keepdims=True)
        gscores.append(m1 + m2)
    gs = jnp.concatenate(gscores, axis=0)
    iota_n = lax.broadcasted_iota(jnp.int32, (N_GROUPS, tile), 0)
    gmask = jnp.zeros((N_GROUPS, tile), jnp.bool_)
    for _ in range(TOPK_GROUPS):
        _, gi = _first_argmax(gs, iota_n, N_GROUPS)
        hit = iota_n == gi
        gmask = gmask | hit
        gs = jnp.where(hit, NEG_INF, gs)
    gmask_f = gmask.astype(F32)
    emask = jnp.concatenate(
        [jnp.broadcast_to(gmask_f[gi:gi + 1, :], (gsz, tile)) for gi in range(N_GROUPS)], axis=0) > 0.5

    masked = jnp.where(emask, choice, NEG_INF)
    idxs, ws = [], []
    for _ in range(TOP_K):
        _, ei = _first_argmax(masked, iota_e, N_EXPERTS)
        hit = iota_e == ei
        idxs.append(ei)
        ws.append(jnp.sum(jnp.where(hit, scores, 0.0), axis=0, keepdims=True))
        masked = jnp.where(hit, NEG_INF, masked)
    wsum = ws[0]
    for w in ws[1:]:
        wsum = wsum + w
    eidx_ref[...] = jnp.concatenate(idxs, axis=0)
    wts_ref[...] = jnp.concatenate(ws, axis=0) / wsum * ROUTED_SCALE


def _router(x, wr_t, bias_col):
    n = x.shape[0]
    return pl.pallas_call(
        _router_kernel,
        grid=(n // ROUTE_T,),
        in_specs=[pl.BlockSpec((ROUTE_T, D_MODEL), lambda i: (i, 0)),
                  pl.BlockSpec((N_EXPERTS, D_MODEL), lambda i: (0, 0)),
                  pl.BlockSpec((N_EXPERTS, 1), lambda i: (0, 0))],
        out_specs=[pl.BlockSpec((TOP_K, ROUTE_T), lambda i: (0, i)),
                   pl.BlockSpec((TOP_K, ROUTE_T), lambda i: (0, i))],
        out_shape=[jax.ShapeDtypeStruct((TOP_K, n), jnp.int32),
                   jax.ShapeDtypeStruct((TOP_K, n), F32)],
        compiler_params=_cparams("parallel"),
        name="router",
    )(x, wr_t, bias_col)


CHUNKS = D_MODEL // LANES
RING = 3


def _tile_copy(src_hbm, row, dst_buf, slot, r, sem, count):
    src = src_hbm.at[pl.ds(pl.multiple_of(row * CHUNKS, CHUNKS), CHUNKS)]
    dst = dst_buf.at[pl.ds(pl.multiple_of((slot * count + r) * CHUNKS, CHUNKS), CHUNKS)]
    return pltpu.make_async_copy(src, dst, sem.at[slot])


def _start_tiles(idx_ref, src_hbm, dst_buf, slot, sem, count):
    for r in range(count):
        _tile_copy(src_hbm, idx_ref[0, 0, r], dst_buf, slot, r, sem, count).start(priority=r % 2)


def _wait_tiles(src_hbm, dst_buf, slot, sem, count):
    def body(r, carry):
        _tile_copy(src_hbm, 0, dst_buf, slot, r, sem, count).wait()
        return carry
    lax.fori_loop(0, count, body, 0, unroll=8)


def _rows_from_tiles(buf, slot, first, count):
    tiles = buf.shape[0] // (RING * CHUNKS)
    base = pl.multiple_of(slot * (tiles * CHUNKS), tiles * CHUNKS) + first * CHUNKS
    return jnp.concatenate([buf[pl.ds(base + c, count, stride=CHUNKS), :] for c in range(CHUNKS)], axis=1)


def _ring_step(i, idx_cur, idx_nxt, idx_ahead, src_hbm, buf, sem, count, compute):
    nsteps = pl.num_programs(0)
    slot = lax.rem(i, RING)

    @pl.when(i == 0)
    def _():
        _start_tiles(idx_cur, src_hbm, buf, 0, sem, count)
        _start_tiles(idx_nxt, src_hbm, buf, 1, sem, count)

    _wait_tiles(src_hbm, buf, slot, sem, count)
    compute(slot)
    _start_tiles(idx_ahead, src_hbm, buf, lax.rem(i + 2, RING), sem, count)

    @pl.when(i == nsteps - 1)
    def _():
        _wait_tiles(src_hbm, buf, lax.rem(i + 1, RING), sem, count)
        _wait_tiles(src_hbm, buf, lax.rem(i + 2, RING), sem, count)


def _ring_specs(nsteps, count, nprefetch):
    def spec(ahead):
        if nprefetch:
            f = lambda i, *_: (jnp.minimum(i + ahead, nsteps - 1), 0, 0)
        else:
            f = lambda i: (jnp.minimum(i + ahead, nsteps - 1), 0, 0)
        return pl.BlockSpec((1, 1, count), f, memory_space=pltpu.SMEM)
    return [spec(0), spec(1), spec(2)]


def _expert_kernel(blk_e_ref, tok_cur, tok_nxt, tok_ahead, x_hbm, wg_ref, wu_ref, wd_ref,
                   y_ref, xbuf, sem):
    del blk_e_ref

    def compute(slot):
        xb = _rows_from_tiles(xbuf, slot, 0, EXP_BLK).astype(BF16)
        hmid = _silu(_dot(xb, wg_ref[...])) * _dot(xb, wu_ref[...])
        y = _dot(hmid.astype(BF16), wd_ref[...])
        for c in range(CHUNKS):
            y_ref[pl.ds(c, EXP_BLK, stride=CHUNKS), :] = y[:, c * LANES:(c + 1) * LANES]

    _ring_step(pl.program_id(0), tok_cur, tok_nxt, tok_ahead, x_hbm, xbuf, sem, EXP_BLK, compute)


def _experts(blk_e, tok_pad, x_tiles, wg, wu, wd):
    nblk = blk_e.shape[0]
    assert nblk >= RING
    tok3 = tok_pad.reshape(nblk, 1, EXP_BLK)
    wmap = lambda i, be: (be[i], 0, 0)
    grid_spec = pltpu.PrefetchScalarGridSpec(
        num_scalar_prefetch=1,
        grid=(nblk,),
        in_specs=_ring_specs(nblk, EXP_BLK, 1) + [
            pl.BlockSpec(memory_space=pl.ANY),
            pl.BlockSpec((None, D_MODEL, EXPERT_DIM), wmap),
            pl.BlockSpec((None, D_MODEL, EXPERT_DIM), wmap),
            pl.BlockSpec((None, EXPERT_DIM, D_MODEL), wmap),
        ],
        out_specs=pl.BlockSpec((EXP_BLK * CHUNKS, LANES), lambda i, be: (i, 0)),
        scratch_shapes=[pltpu.VMEM((RING * EXP_BLK * CHUNKS, LANES), F32), pltpu.SemaphoreType.DMA((RING,))],
    )
    return pl.pallas_call(
        _expert_kernel,
        grid_spec=grid_spec,
        out_shape=jax.ShapeDtypeStruct((nblk * EXP_BLK * CHUNKS, LANES), F32),
        compiler_params=_cparams("arbitrary"),
        name="experts",
    )(blk_e, tok3, tok3, tok3, x_tiles, wg, wu, wd)


def _combine_kernel(dst_cur, dst_nxt, dst_ahead, y_hbm, wts_ref, x_ref, wg_ref, wu_ref, wd_ref,
                    g_ref, b_ref, xo_ref, ybuf, sem):
    def compute(slot):
        x = x_ref[...]
        xb = x.astype(BF16)
        shared = _dot((_silu(_dot(xb, wg_ref[...])) * _dot(xb, wu_ref[...])).astype(BF16), wd_ref[...])
        wts = wts_ref[...]
        routed = None
        for k in range(TOP_K):
            term = _rows_from_tiles(ybuf, slot, k * COMB_T, COMB_T) * wts[:, k:k + 1]
            routed = term if routed is None else routed + term
        xo_ref[...] = _layer_norm(DEEPNORM_ALPHA * x + (routed + shared), g_ref[...], b_ref[...])

    _ring_step(pl.program_id(0), dst_cur, dst_nxt, dst_ahead, y_hbm, ybuf, sem, TOP_K * COMB_T, compute)


def _combine(dest, y_tiles, wts, x, wg, wu, wd, g, b):
    n = x.shape[0]
    nt = n // COMB_T
    assert nt >= RING
    count = TOP_K * COMB_T
    dest3 = dest.reshape(nt, 1, count)
    row = lambda i: (i, 0)
    const = lambda i: (0, 0)
    return pl.pallas_call(
        _combine_kernel,
        grid=(nt,),
        in_specs=_ring_specs(nt, count, 0) + [
            pl.BlockSpec(memory_space=pl.ANY),
            pl.BlockSpec((COMB_T, TOP_K), row),
            pl.BlockSpec((COMB_T, D_MODEL), row),
            pl.BlockSpec((D_MODEL, EXPERT_DIM), const),
            pl.BlockSpec((D_MODEL, EXPERT_DIM), const),
            pl.BlockSpec((EXPERT_DIM, D_MODEL), const),
            pl.BlockSpec((1, D_MODEL), const),
            pl.BlockSpec((1, D_MODEL), const),
        ],
        out_specs=pl.BlockSpec((COMB_T, D_MODEL), row),
        out_shape=jax.ShapeDtypeStruct((n, D_MODEL), F32),
        scratch_shapes=[pltpu.VMEM((RING * count * CHUNKS, LANES), F32), pltpu.SemaphoreType.DMA((RING,))],
        compiler_params=_cparams("arbitrary"),
        name="combine",
    )(dest3, dest3, dest3, y_tiles, wts, x, wg, wu, wd, g, b)


def _dispatch_plan(eidx, n):
    nk = n * TOP_K
    nblk = -(-(nk + N_EXPERTS * (EXP_BLK - 1)) // EXP_BLK)
    i32 = jnp.int32
    experts = jnp.arange(N_EXPERTS, dtype=i32)
    flat_e = eidx.reshape(-1).astype(i32)
    e_sorted, order = lax.sort_key_val(flat_e, jnp.arange(nk, dtype=i32))
    onehot_sorted = e_sorted[:, None] == experts[None, :]
    counts = jnp.sum(onehot_sorted, axis=0, dtype=i32)
    padded = (counts + EXP_BLK - 1) // EXP_BLK * EXP_BLK
    start = jnp.cumsum(counts) - counts
    pend = jnp.cumsum(padded)
    pstart = pend - padded
    shift = jnp.sum(jnp.where(onehot_sorted, (pstart - start)[None, :], 0), axis=1, dtype=i32)
    dest_sorted = jnp.arange(nk, dtype=i32) + shift
    _, dest = lax.sort_key_val(order, dest_sorted)
    blk_start = jnp.arange(nblk, dtype=i32) * EXP_BLK
    blk_e = jnp.minimum(jnp.sum(blk_start[:, None] >= pend[None, :], axis=1, dtype=i32), N_EXPERTS - 1)
    blk_onehot = blk_e[:, None] == experts[None, :]
    pick = lambda table: jnp.sum(jnp.where(blk_onehot, table[None, :], 0), axis=1, dtype=i32)
    q = (blk_start - pick(pstart))[:, None] + jnp.arange(EXP_BLK, dtype=i32)[None, :]
    valid = (q >= 0) & (q < pick(counts)[:, None])
    src = jnp.clip(pick(start)[:, None] + q, 0, nk - 1)
    tok_pad = jnp.where(valid, order[src] // TOP_K, 0).astype(i32).reshape(-1)
    return blk_e, tok_pad, dest.reshape(n, TOP_K)


def _moe(x, wr_t, bias_col, wg, wu, wd, wgs, wus, wds, g, b):
    n = x.shape[0]
    eidx_t, wts_t = _router(x, wr_t, bias_col)
    eidx = eidx_t.T
    wts = wts_t.T
    blk_e, tok_pad, dest = _dispatch_plan(eidx, n)
    y = _experts(blk_e, tok_pad, x.reshape(n * CHUNKS, LANES), wg, wu, wd)
    dest_tiles = dest.reshape(n // COMB_T, COMB_T, TOP_K).transpose(0, 2, 1)
    return _combine(dest_tiles, y, wts, x, wgs, wus, wds, g, b)


def kernel(x, w_in, dn_conv_w, dn_a_log, dn_dt_bias, dn_norm_w, w_branch_a, w_branch_b, w_out,
           ln1_g, ln1_b, w_router, router_bias, w_gate_e, w_up_e, w_down_e,
           w_gate_s, w_up_s, w_down_s, ln2_g, ln2_b):
    bsz, seq, d = x.shape
    n = bsz * seq
    assert d == D_MODEL and seq % (2 * MOBA_BLOCK) == 0
    assert n % MM_TM == 0 and seq % PREP_T == 0 and n % MIX_T == 0 and n % ROUTE_T == 0
    depth = w_in.shape[0]
    xf = x.reshape(n, d).astype(F32)
    ab_lo = 7 * D_MODEL
    ab_hi = ab_lo + 2 * N_HEADS
    vec = lambda t: t.astype(F32).reshape(1, D_MODEL)
    for l in range(depth):
        w_main = jnp.concatenate([w_in[l, :, :ab_lo], w_in[l, :, ab_hi:]], axis=1).astype(BF16)
        w_ab = jnp.pad(w_in[l, :, ab_lo:ab_hi], ((0, 0), (0, LANES - 2 * N_HEADS))).astype(BF16)
        xb = xf.astype(BF16)
        proj = _matmul(xb, w_main, BF16, MM_TM, MM_TN)
        ab = _matmul(xb, w_ab, F32, MM_TM, LANES)
        ya = _moba(proj, bsz, seq)
        yb = _gdn(proj, ab, dn_conv_w[l], dn_a_log[l], dn_dt_bias[l], dn_norm_w[l], bsz, seq)
        x1 = _mix(ya, yb, proj, xf, w_branch_a[l].astype(BF16), w_branch_b[l].astype(BF16),
                  w_out[l].astype(BF16), vec(ln1_g[l]), vec(ln1_b[l]))
        xf = _moe(x1, w_router[l].astype(F32).T, router_bias[l].astype(F32).reshape(N_EXPERTS, 1),
                  w_gate_e[l].astype(BF16), w_up_e[l].astype(BF16), w_down_e[l].astype(BF16),
                  w_gate_s[l].astype(BF16), w_up_s[l].astype(BF16), w_down_s[l].astype(BF16),
                  vec(ln2_g[l]), vec(ln2_b[l]))
    return xf.reshape(bsz, seq, d).astype(x.dtype)
```
